```python
import jax, jax.numpy as jnp
from jax import lax
import numpy as np

D_MODEL = 1024
BATCH = 8
SEQ = 2048
DEPTH = 2
DEC_BATCH = 128
DEC_SEQ = 8
PAST_LEN = 2048
PAGE_SIZE = 128

N_HEADS_A = 8
HEAD_DIM = 64
D_ATT = N_HEADS_A * HEAD_DIM
ROT_DIM = HEAD_DIM // 4
ROPE_THETA = 500000.0
PATTERNS = ((128, 1), (512, 4), (2048, 16))
WIN_MAX = 2048
Q_BLOCK = 64
D_RNN = 512
RNN_BLOCKS = 8
RNN_BW = D_RNN // RNN_BLOCKS
CONV_W = 4
LRU_C = 8.0
N_HEADS_C = 8
HEAD_C = 64
D_RWKV = N_HEADS_C * HEAD_C
LORA_W = 64
LORA_A = 64
LORA_G = 128
C_COLS = 3 * D_RWKV + LORA_W + LORA_A + LORA_G
GN_EPS = 64e-5
N_BRANCH = 3
IN_COLS = 3 * D_ATT + 2 * D_RNN + C_COLS + N_BRANCH * D_MODEL
SPLITS = (D_ATT, 2 * D_ATT, 3 * D_ATT, 3 * D_ATT + D_RNN, 3 * D_ATT + 2 * D_RNN, 3 * D_ATT + 2 * D_RNN + C_COLS)
N_GROUPS = 4
EXP_PER_GROUP = 4
N_EXPERTS = N_GROUPS * EXP_PER_GROUP
TOP_K = 2
D_EXPERT = 512
MOE_CHUNK = 1024
EPS = 1e-6

kernel_name = 'hybrid_dilated_rglru_rwkv7_hmoe_step'

F32 = jnp.float32


def rmsnorm(x, w):
    xf = x.astype(F32)
    y = xf * lax.rsqrt(jnp.mean(xf * xf, axis=-1, keepdims=True) + EPS) * w.astype(F32)
    return y.astype(x.dtype)


def partial_rope(x, pos):
    half = ROT_DIM // 2
    inv = ROPE_THETA ** (-jnp.arange(half, dtype=F32) / half)
    ang = pos.astype(F32)[:, None] * inv[None, :]
    cos = jnp.cos(ang)[None, :, None, :]
    sin = jnp.sin(ang)[None, :, None, :]
    xf = x.astype(F32)
    x1 = xf[..., :half]
    x2 = xf[..., half:ROT_DIM]
    out = jnp.concatenate([x1 * cos - x2 * sin, x2 * cos + x1 * sin, xf[..., ROT_DIM:]], axis=-1)
    return out.astype(x.dtype)


def dilated_attention(q, k_all, v_all, q_pos, k_start):
    tk = k_all.shape[1]
    qf = q.astype(F32) * (HEAD_DIM ** -0.5)
    ms, ss, os_ = [], [], []
    for window, dil in PATTERNS:
        n_keys = window // dil + 1
        p = q_pos[:, None] - dil * jnp.arange(n_keys, dtype=jnp.int32)[None, :]
        idx = p - k_start
        valid = (p >= 0) & (idx >= 0)
        idx = jnp.clip(idx, 0, tk - 1)
        kg = jnp.take(k_all, idx, axis=1).astype(F32)
        vg = jnp.take(v_all, idx, axis=1).astype(F32)
        sc = jnp.einsum('bqhd,bqjhd->bhqj', qf, kg)
        sc = jnp.where(valid[None, None], sc, -jnp.inf)
        m = jnp.max(sc, axis=-1)
        e = jnp.exp(sc - m[..., None])
        s = jnp.sum(e, axis=-1)
        o = jnp.einsum('bhqj,bqjhd->bhqd', e, vg) / s[..., None]
        ms.append(m)
        ss.append(s)
        os_.append(o)
    m_all = jnp.stack(ms)
    s_all = jnp.stack(ss)
    o_all = jnp.stack(os_)
    wgt = s_all * jnp.exp(m_all - jnp.max(m_all, axis=0, keepdims=True))
    out = jnp.sum(wgt[..., None] * o_all, axis=0) / jnp.sum(wgt, axis=0)[..., None]
    return jnp.transpose(out, (0, 2, 1, 3))


def windowed_attention(q, k_all, v_all, pos0, k_start):
    b, tq = q.shape[0], q.shape[1]
    if tq <= Q_BLOCK or tq % Q_BLOCK != 0:
        return dilated_attention(q, k_all, v_all, pos0 + jnp.arange(tq, dtype=jnp.int32), k_start)
    nb = tq // Q_BLOCK

    def body(i):
        start = i * Q_BLOCK
        qb = lax.dynamic_slice_in_dim(q, start, Q_BLOCK, axis=1)
        return dilated_attention(qb, k_all, v_all, pos0 + start + jnp.arange(Q_BLOCK, dtype=jnp.int32), k_start)

    out = lax.map(body, jnp.arange(nb, dtype=jnp.int32))
    return jnp.moveaxis(out, 0, 1).reshape(b, tq, N_HEADS_A, HEAD_DIM)


def _lru_combine(c1, c2):
    a1, b1 = c1
    a2, b2 = c2
    return a1 * a2, a2 * b1 + b2


def rglru_mixer(xrec, gate, conv_buf, h0, conv_w, conv_b, wa, ba, wx, bx, lam):
    b, t = xrec.shape[0], xrec.shape[1]
    xpad = jnp.concatenate([conv_buf.astype(F32), xrec.astype(F32)], axis=1)
    xc = conv_b.astype(F32)
    for i in range(CONV_W):
        xc = xc + conv_w[i].astype(F32) * xpad[:, i:i + t]
    xb = xc.reshape(b, t, RNN_BLOCKS, RNN_BW)
    r = jax.nn.sigmoid(jnp.einsum('btgi,gij->btgj', xb, wa.astype(F32)).reshape(b, t, D_RNN) + ba.astype(F32))
    ig = jax.nn.sigmoid(jnp.einsum('btgi,gij->btgj', xb, wx.astype(F32)).reshape(b, t, D_RNN) + bx.astype(F32))
    log_a = -LRU_C * r * jax.nn.softplus(-lam.astype(F32))
    a = jnp.exp(log_a)
    bt = jnp.sqrt(-jnp.expm1(2.0 * log_a)) * (ig * xc)
    a_cum, b_acc = lax.associative_scan(_lru_combine, (a, bt), axis=1)
    h = a_cum * h0.astype(F32)[:, None, :] + b_acc
    y = h * jax.nn.gelu(gate.astype(F32))
    return y, xpad[:, -(CONV_W - 1):], h[:, -1]


def wkv_scan(s0, r, w, k, v, kk, a):
    def step(s, inp):
        r_t, w_t, k_t, v_t, kk_t, a_t = inp
        sa = jnp.einsum('bhvk,bhk->bhv', s, -kk_t)
        s = s * w_t[:, :, None, :] + sa[..., :, None] * (kk_t * a_t)[:, :, None, :] + v_t[..., :, None] * k_t[:, :, None, :]
        y = jnp.einsum('bhvk,bhk->bhv', s, r_t)
        return s, y
    xs = tuple(jnp.moveaxis(z, 1, 0) for z in (r, w, k, v, kk, a))
    s_last, ys = lax.scan(step, s0, xs)
    return s_last, jnp.moveaxis(ys, 0, 1)


def rwkv7_mixer(c, shift_prev, s0, mu, w0, w2, a0, a2, g2, k_k, k_a, r_k, ln_w, ln_b):
    b, t = c.shape[0], c.shape[1]
    cf = c.astype(F32)
    prev = jnp.concatenate([shift_prev.astype(F32)[:, None, :], cf[:, :-1]], axis=1)
    cm = cf + (prev - cf) * mu.astype(F32)
    r, k, v, w_lo, a_lo, g_lo = jnp.split(cm, [D_RWKV, 2 * D_RWKV, 3 * D_RWKV, 3 * D_RWKV + LORA_W, 3 * D_RWKV + LORA_W + LORA_A], axis=-1)
    w = -jax.nn.softplus(-(w0.astype(F32) + jnp.tanh(w_lo) @ w2.astype(F32))) - 0.5
    decay = jnp.exp(-jnp.exp(w))
    a = jax.nn.sigmoid(a0.astype(F32) + a_lo @ a2.astype(F32))
    g = jax.nn.sigmoid(g_lo) @ g2.astype(F32)
    heads = lambda z: z.reshape(b, t, N_HEADS_C, HEAD_C)
    kk = heads(k * k_k.astype(F32))
    kk = kk / jnp.maximum(jnp.sqrt(jnp.sum(kk * kk, axis=-1, keepdims=True)), 1e-12)
    k = k * (1.0 + (a - 1.0) * k_a.astype(F32))
    rh, kh, vh = heads(r), heads(k), heads(v)
    s_last, y = wkv_scan(s0.astype(F32), rh, heads(decay), kh, vh, kk, heads(a))
    mean = jnp.mean(y, axis=-1, keepdims=True)
    var = jnp.mean(jnp.square(y - mean), axis=-1, keepdims=True)
    y = ((y - mean) * lax.rsqrt(var + GN_EPS)).reshape(b, t, D_RWKV) * ln_w.astype(F32) + ln_b.astype(F32)
    bonus = jnp.sum(rh * kh * r_k.astype(F32), axis=-1, keepdims=True) * vh
    out = (y + bonus.reshape(b, t, D_RWKV)) * g
    return out, c[:, -1], s_last


def hier_moe(x, wg_group, bg_group, wg_exp, bg_exp, w1, w3, w2):
    b, t, d = x.shape
    n = b * t
    xt = jnp.pad(x.reshape(n, d), ((0, (-n) % MOE_CHUNK), (0, 0)))
    chunks = xt.reshape(-1, MOE_CHUNK, d)
    w1f, w3f, w2f = w1.astype(F32), w3.astype(F32), w2.astype(F32)

    def body(xc):
        xf = xc.astype(F32)
        lg = xf @ wg_group.astype(F32) + bg_group.astype(F32)
        pg = jax.nn.softmax(lg, axis=-1)
        gsel = jnp.argmax(lg, axis=-1)
        le = (xf @ wg_exp.astype(F32) + bg_exp.astype(F32)).reshape(-1, N_GROUPS, EXP_PER_GROUP)
        le_sel = jnp.take_along_axis(le, gsel[:, None, None], axis=1)[:, 0]
        top_v, top_i = lax.top_k(le_sel, TOP_K)
        wt = jax.nn.softmax(top_v, axis=-1) * jnp.take_along_axis(pg, gsel[:, None], axis=1)
        within = jnp.sum(jax.nn.one_hot(top_i, EXP_PER_GROUP, dtype=F32) * wt[..., None], axis=1)
        comb = (jax.nn.one_hot(gsel, N_GROUPS, dtype=F32)[:, :, None] * within[:, None, :]).reshape(-1, N_EXPERTS)
        h = jax.nn.silu(jnp.einsum('cd,edf->cef', xf, w1f)) * jnp.einsum('cd,edf->cef', xf, w3f)
        return jnp.einsum('cef,efd->cd', h * comb[..., None], w2f)

    out = lax.map(body, chunks).reshape(-1, d)[:n]
    return out.reshape(b, t, d)


def hybrid_layer(x, pos0, k_past, v_past, k_start, conv_buf, h0, shift_prev, s0, p):
    b, t = x.shape[0], x.shape[1]
    xn = rmsnorm(x, p['norm1_w'])
    proj = xn.astype(F32) @ p['w_in'].astype(F32)
    q, k, v, xrec, grnn, c_cols, gl = jnp.split(proj, list(SPLITS), axis=-1)
    pos = pos0 + jnp.arange(t, dtype=jnp.int32)
    q = partial_rope(rmsnorm(q.reshape(b, t, N_HEADS_A, HEAD_DIM), p['q_norm_w']), pos)
    k = partial_rope(rmsnorm(k.reshape(b, t, N_HEADS_A, HEAD_DIM), p['k_norm_w']), pos)
    v = v.reshape(b, t, N_HEADS_A, HEAD_DIM)
    k_all = jnp.concatenate([k_past.astype(F32), k], axis=1)
    v_all = jnp.concatenate([v_past.astype(F32), v], axis=1)
    o_a = windowed_attention(q, k_all, v_all, pos0, k_start).reshape(b, t, D_ATT)
    o_b, conv_tail, h_last = rglru_mixer(xrec, grnn, conv_buf, h0, p['conv_w'], p['conv_b'], p['rg_wa'], p['rg_ba'], p['rg_wx'], p['rg_bx'], p['rg_lambda'])
    o_c, shift_last, s_last = rwkv7_mixer(c_cols, shift_prev, s0, p['rwkv_mu'], p['rwkv_w0'], p['rwkv_w2'], p['rwkv_a0'], p['rwkv_a2'], p['rwkv_g2'], p['rwkv_kk'], p['rwkv_ka'], p['rwkv_rk'], p['rwkv_lnw'], p['rwkv_lnb'])
    gates = jax.nn.sigmoid(gl.reshape(b, t, N_BRANCH, D_MODEL) + p['b_merge'].astype(F32))
    merged = (gates[:, :, 0] * (o_a @ p['w_branch_a'].astype(F32))
              + gates[:, :, 1] * (o_b @ p['w_branch_b'].astype(F32))
              + gates[:, :, 2] * (o_c @ p['w_branch_c'].astype(F32)))
    x = x + (merged @ p['w_out'].astype(F32)).astype(x.dtype)
    ffn = hier_moe(rmsnorm(x, p['norm2_w']), p['moe_wg_group'], p['moe_bg_group'], p['moe_wg_exp'], p['moe_bg_exp'], p['moe_w1'], p['moe_w3'], p['moe_w2'])
    x = x + ffn.astype(x.dtype)
    return x, (k, v, conv_tail, h_last, shift_last, s_last)


def setup_inputs(seed: int = 0) -> dict:
    key = jax.random.key(seed)
    ks = iter(jax.random.split(key, 64))

    def nrm(shape, scale):
        return scale * jax.random.normal(next(ks), shape, F32)

    def unif(shape, lo, hi):
        return jax.random.uniform(next(ks), shape, F32, lo, hi)

    L = DEPTH
    win_buf = min(WIN_MAX, PAST_LEN)
    a_c = unif((L, D_RNN), 0.9, 0.999)
    a_lru = a_c ** (1.0 / LRU_C)
    return {
        'x_prompt': nrm((BATCH, SEQ, D_MODEL), 1.0),
        'x_sample': nrm((DEC_BATCH, DEC_SEQ, D_MODEL), 1.0),
        'cache_k_win': nrm((L, DEC_BATCH, win_buf, N_HEADS_A, HEAD_DIM), 1.0),
        'cache_v_win': nrm((L, DEC_BATCH, win_buf, N_HEADS_A, HEAD_DIM), 1.0),
        'state_conv': nrm((L, DEC_BATCH, CONV_W - 1, D_RNN), 1.0),
        'state_h': nrm((L, DEC_BATCH, D_RNN), 0.5),
        'state_shift': nrm((L, DEC_BATCH, C_COLS), 1.0),
        'state_wkv': nrm((L, DEC_BATCH, N_HEADS_C, HEAD_C, HEAD_C), 0.5),
        'norm1_w': 1.0 + nrm((L, D_MODEL), 0.05),
        'w_in': nrm((L, D_MODEL, IN_COLS), D_MODEL ** -0.5),
        'q_norm_w': 1.0 + nrm((L, HEAD_DIM), 0.05),
        'k_norm_w': 1.0 + nrm((L, HEAD_DIM), 0.05),
        'conv_w': nrm((L, CONV_W, D_RNN), CONV_W ** -0.5),
        'conv_b': nrm((L, D_RNN), 0.02),
        'rg_wa': nrm((L, RNN_BLOCKS, RNN_BW, RNN_BW), RNN_BW ** -0.5),
        'rg_ba': nrm((L, D_RNN), 0.02),
        'rg_wx': nrm((L, RNN_BLOCKS, RNN_BW, RNN_BW), RNN_BW ** -0.5),
        'rg_bx': nrm((L, D_RNN), 0.02),
        'rg_lambda': jnp.log(a_lru) - jnp.log1p(-a_lru),
        'rwkv_mu': unif((L, C_COLS), 0.0, 1.0),
        'rwkv_w0': unif((L, D_RWKV), -6.0, 1.0),
        'rwkv_w2': nrm((L, LORA_W, D_RWKV), 0.1),
        'rwkv_a0': nrm((L, D_RWKV), 0.5),
        'rwkv_a2': nrm((L, LORA_A, D_RWKV), 0.1),
        'rwkv_g2': nrm((L, LORA_G, D_RWKV), LORA_G ** -0.5),
        'rwkv_kk': 0.85 + nrm((L, D_RWKV), 0.05),
        'rwkv_ka': 1.0 + nrm((L, D_RWKV), 0.05),
        'rwkv_rk': nrm((L, N_HEADS_C, HEAD_C), 0.1),
        'rwkv_lnw': 1.0 + nrm((L, D_RWKV), 0.05),
        'rwkv_lnb': nrm((L, D_RWKV), 0.02),
        'b_merge': nrm((L, N_BRANCH, D_MODEL), 0.1),
        'w_branch_a': nrm((L, D_ATT, D_MODEL), D_ATT ** -0.5),
        'w_branch_b': nrm((L, D_RNN, D_MODEL), D_RNN ** -0.5),
        'w_branch_c': nrm((L, D_RWKV, D_MODEL), D_RWKV ** -0.5),
        'w_out': nrm((L, D_MODEL, D_MODEL), D_MODEL ** -0.5),
        'norm2_w': 1.0 + nrm((L, D_MODEL), 0.05),
        'moe_wg_group': nrm((L, D_MODEL, N_GROUPS), D_MODEL ** -0.5),
        'moe_bg_group': nrm((L, N_GROUPS), 0.01),
        'moe_wg_exp': nrm((L, D_MODEL, N_EXPERTS), D_MODEL ** -0.5),
        'moe_bg_exp': nrm((L, N_EXPERTS), 0.01),
        'moe_w1': nrm((L, N_EXPERTS, D_MODEL, D_EXPERT), D_MODEL ** -0.5),
        'moe_w3': nrm((L, N_EXPERTS, D_MODEL, D_EXPERT), D_MODEL ** -0.5),
        'moe_w2': nrm((L, N_EXPERTS, D_EXPERT, D_MODEL), D_EXPERT ** -0.5),
    }


def reference(x_prompt, x_sample, cache_k_win, cache_v_win, state_conv, state_h, state_shift, state_wkv,
              norm1_w, w_in, q_norm_w, k_norm_w, conv_w, conv_b, rg_wa, rg_ba, rg_wx, rg_bx, rg_lambda,
              rwkv_mu, rwkv_w0, rwkv_w2, rwkv_a0, rwkv_a2, rwkv_g2, rwkv_kk, rwkv_ka, rwkv_rk, rwkv_lnw, rwkv_lnb,
              b_merge, w_branch_a, w_branch_b, w_branch_c, w_out,
              norm2_w, moe_wg_group, moe_bg_group, moe_wg_exp, moe_bg_exp, moe_w1, moe_w3, moe_w2):
    b_p, t_p = x_prompt.shape[0], x_prompt.shape[1]
    win_buf = cache_k_win.shape[2]
    keep = min(WIN_MAX, t_p)
    dt = x_prompt.dtype
    yp, ys = x_prompt, x_sample
    new_p = [[] for _ in range(6)]
    new_s = [[] for _ in range(6)]
    for l in range(DEPTH):
        p = dict(norm1_w=norm1_w[l], w_in=w_in[l], q_norm_w=q_norm_w[l], k_norm_w=k_norm_w[l],
                 conv_w=conv_w[l], conv_b=conv_b[l], rg_wa=rg_wa[l], rg_ba=rg_ba[l], rg_wx=rg_wx[l], rg_bx=rg_bx[l],
                 rg_lambda=rg_lambda[l], rwkv_mu=rwkv_mu[l], rwkv_w0=rwkv_w0[l], rwkv_w2=rwkv_w2[l],
                 rwkv_a0=rwkv_a0[l], rwkv_a2=rwkv_a2[l], rwkv_g2=rwkv_g2[l], rwkv_kk=rwkv_kk[l], rwkv_ka=rwkv_ka[l],
                 rwkv_rk=rwkv_rk[l], rwkv_lnw=rwkv_lnw[l], rwkv_lnb=rwkv_lnb[l], b_merge=b_merge[l],
                 w_branch_a=w_branch_a[l], w_branch_b=w_branch_b[l], w_branch_c=w_branch_c[l], w_out=w_out[l],
                 norm2_w=norm2_w[l], moe_wg_group=moe_wg_group[l], moe_bg_group=moe_bg_group[l],
                 moe_wg_exp=moe_wg_exp[l], moe_bg_exp=moe_bg_exp[l], moe_w1=moe_w1[l], moe_w3=moe_w3[l], moe_w2=moe_w2[l])
        yp, st_p = hybrid_layer(
            yp, 0,
            jnp.zeros((b_p, 0, N_HEADS_A, HEAD_DIM), dt), jnp.zeros((b_p, 0, N_HEADS_A, HEAD_DIM), dt), 0,
            jnp.zeros((b_p, CONV_W - 1, D_RNN), dt), jnp.zeros((b_p, D_RNN), dt),
            jnp.zeros((b_p, C_COLS), dt), jnp.zeros((b_p, N_HEADS_C, HEAD_C, HEAD_C), dt), p)
        ys, st_s = hybrid_layer(
            ys, PAST_LEN, cache_k_win[l], cache_v_win[l], PAST_LEN - win_buf,
            state_conv[l], state_h[l], state_shift[l], state_wkv[l], p)
        new_p[0].append(st_p[0][:, t_p - keep:])
        new_p[1].append(st_p[1][:, t_p - keep:])
        for j in range(2, 6):
            new_p[j].append(st_p[j])
        for j in range(6):
            new_s[j].append(st_s[j])
    k_win_prompt = jnp.stack(new_p[0])
    v_win_prompt = jnp.stack(new_p[1])
    conv_prompt = jnp.stack(new_p[2])
    h_prompt = jnp.stack(new_p[3])
    shift_prompt = jnp.stack(new_p[4])
    wkv_prompt = jnp.stack(new_p[5])
    k_rows_sample = jnp.stack(new_s[0])
    v_rows_sample = jnp.stack(new_s[1])
    conv_sample = jnp.stack(new_s[2])
    h_sample = jnp.stack(new_s[3])
    shift_sample = jnp.stack(new_s[4])
    wkv_sample = jnp.stack(new_s[5])
    return (yp, ys, k_win_prompt, v_win_prompt, conv_prompt, h_prompt, shift_prompt, wkv_prompt,
            k_rows_sample, v_rows_sample, conv_sample, h_sample, shift_sample, wkv_sample)
```

```python
import functools

import numpy as np
import jax
import jax.numpy as jnp
from jax import lax
from jax.experimental import pallas as pl
from jax.experimental.pallas import tpu as pltpu

F32 = jnp.float32
BF16 = jnp.bfloat16

D_MODEL = 1024
N_HEADS = 8
HEAD_DIM = 64
D_ATT = N_HEADS * HEAD_DIM
ROT_DIM = HEAD_DIM // 4
ROPE_THETA = 500000.0
PATTERNS = ((128, 1), (512, 4), (2048, 16))
CONV_W = 4
LRU_C = 8.0
LORA_W, LORA_A, LORA_G = 64, 64, 128
C_COLS = 3 * D_ATT + LORA_W + LORA_A + LORA_G
N_BRANCH = 3
GL_COLS = N_BRANCH * D_MODEL
N_GROUPS, EXP_PER_GROUP, N_EXPERTS = 4, 4, 16
D_EXPERT = 512
EPS = 1e-6
GN_EPS = 64e-5

LANES = 128
SUBLANES = 8
VMEM_LIMIT = 56 * 1024 * 1024
HEAD_PAIRS = D_ATT // LANES
NEG_BIG = -1e30


def _cparams(sem):
    return pltpu.CompilerParams(dimension_semantics=sem, vmem_limit_bytes=VMEM_LIMIT)


def _dot(a, b):
    return jnp.dot(a.astype(BF16), b.astype(BF16), preferred_element_type=F32)


def _dot_nt(a, b):
    return lax.dot_general(a.astype(BF16), b.astype(BF16), (((1,), (1,)), ((), ())),
                           preferred_element_type=F32)


def _dot_split(x, w_bf16, passes):
    acc = None
    r = x
    for i in range(passes):
        p = r.astype(BF16)
        t = jnp.dot(p, w_bf16, preferred_element_type=F32)
        acc = t if acc is None else acc + t
        if i + 1 < passes:
            r = r - p.astype(F32)
    return acc


def _split_dot_left(w_exact_bf16, x, passes):
    acc = None
    r = x
    for i in range(passes):
        p = r.astype(BF16)
        t = jnp.dot(w_exact_bf16, p, preferred_element_type=F32)
        acc = t if acc is None else acc + t
        if i + 1 < passes:
            r = r - p.astype(F32)
    return acc


def _softplus(x):
    return jnp.maximum(x, 0.0) + jnp.log1p(jnp.exp(-jnp.abs(x)))


def _head_ones():
    h = np.arange(D_ATT) // HEAD_DIM
    return jnp.asarray((h[:, None] == h[None, :]).astype(np.float32), dtype=BF16)


def _inproj_kernel(x_ref, nw_ref, w_ref, qkv_ref, rg_ref, cc_ref, gl_ref):
    x = x_ref[...]
    xn = x * lax.rsqrt(jnp.mean(x * x, axis=-1, keepdims=True) + EPS) * nw_ref[...]
    xb = xn.astype(BF16)
    off = 0
    for ref in (qkv_ref, rg_ref, cc_ref, gl_ref):
        n = ref.shape[-1]
        ref[...] = jnp.dot(xb, w_ref[:, off:off + n], preferred_element_type=F32)
        off += n


def _inproj(x, norm_w, w_in_bf16, tm=256):
    n = x.shape[0]
    widths = (3 * D_ATT, 2 * D_ATT, C_COLS, GL_COLS)
    in_cols = sum(widths)
    return pl.pallas_call(
        _inproj_kernel,
        grid=(n // tm,),
        in_specs=[pl.BlockSpec((tm, D_MODEL), lambda i: (i, 0)),
                  pl.BlockSpec((1, D_MODEL), lambda i: (0, 0)),
                  pl.BlockSpec((D_MODEL, in_cols), lambda i: (0, 0))],
        out_specs=[pl.BlockSpec((tm, w), lambda i: (i, 0)) for w in widths],
        out_shape=[jax.ShapeDtypeStruct((n, w), F32) for w in widths],
        compiler_params=_cparams(("parallel",)),
        name="inproj",
    )(x, norm_w.reshape(1, D_MODEL), w_in_bf16)


def _rope_tables(pos):
    half = ROT_DIM // 2
    inv = ROPE_THETA ** (-jnp.arange(half, dtype=F32) / half)
    ang = pos.astype(F32)[:, None] * inv[None, :]
    cos, sin = jnp.cos(ang), jnp.sin(ang)
    t = pos.shape[0]
    rest = HEAD_DIM - ROT_DIM
    c_head = jnp.concatenate([cos, cos, jnp.ones((t, rest), F32)], axis=1)
    s_prev = jnp.concatenate([jnp.zeros((t, half), F32), sin, jnp.zeros((t, rest), F32)], axis=1)
    s_next = jnp.concatenate([-sin, jnp.zeros((t, half), F32), jnp.zeros((t, rest), F32)], axis=1)
    tile = lambda z: jnp.tile(z, (1, N_HEADS))
    return tile(c_head), tile(s_prev), tile(s_next)


def _qkprep_kernel(q_ref, k_ref, cos_ref, sp_ref, sn_ref, qw_ref, kw_ref, ones_ref, qo_ref, ko_ref):
    half = ROT_DIM // 2
    cos, sp, sn = cos_ref[...], sp_ref[...], sn_ref[...]
    for src, w_ref, dst, scale in ((q_ref, qw_ref, qo_ref, HEAD_DIM ** -0.5), (k_ref, kw_ref, ko_ref, None)):
        x = src[...]
        ms = _dot_split(x * x, ones_ref[...], 2) * (1.0 / HEAD_DIM)
        xn = x * lax.rsqrt(ms + EPS) * w_ref[...]
        y = xn * cos + pltpu.roll(xn, half, 1) * sp + pltpu.roll(xn, D_ATT - half, 1) * sn
        dst[...] = y if scale is None else y * scale


def _qkprep(qkv, row0, nrows, tables, table_blocks, qw, kw, tm):
    cos, sp, sn = tables
    rb0 = row0 // tm
    tab_spec = pl.BlockSpec((tm, D_ATT), lambda i: (i % table_blocks, 0))
    vec = pl.BlockSpec((1, D_ATT), lambda i: (0, 0))
    return pl.pallas_call(
        _qkprep_kernel,
        grid=(nrows // tm,),
        in_specs=[pl.BlockSpec((tm, D_ATT), lambda i: (rb0 + i, 0)),
                  pl.BlockSpec((tm, D_ATT), lambda i: (rb0 + i, 1)),
                  tab_spec, tab_spec, tab_spec, vec, vec,
                  pl.BlockSpec((D_ATT, D_ATT), lambda i: (0, 0))],
        out_specs=[pl.BlockSpec((tm, D_ATT), lambda i: (i, 0))] * 2,
        out_shape=[jax.ShapeDtypeStruct((nrows, D_ATT), F32)] * 2,
        compiler_params=_cparams(("parallel",)),
        name="qkprep",
    )(qkv, qkv, cos, sp, sn, jnp.tile(qw, N_HEADS).reshape(1, D_ATT), jnp.tile(kw, N_HEADS).reshape(1, D_ATT),
      _head_ones())


def _multiplicity(q_pos, k_pos):
    d = q_pos[:, None] - k_pos[None, :]
    c = np.zeros(d.shape, np.float32)
    for window, dil in PATTERNS:
        c += ((d >= 0) & (d <= window) & (d % dil == 0) & (k_pos[None, :] >= 0)).astype(np.float32)
    return c


def _attn_prompt_kernel(q_ref, k_ref, v_ref, mask_ref, o_ref, *, tq):
    qi = pl.program_id(1)
    lane = lax.broadcasted_iota(jnp.int32, (1, LANES), 1)
    first = lane < HEAD_DIM
    for hp in range(HEAD_PAIRS):
        sl = slice(hp * LANES, (hp + 1) * LANES)
        qp = q_ref[:, sl]
        outs = []
        for use_first in (True, False):
            sel = first if use_first else jnp.logical_not(first)
            qm = jnp.where(sel, qp, 0.0).astype(BF16)

            def body(kj, carry, qm=qm, sl=sl):
                m, l, acc = carry
                rows = pl.ds(pl.multiple_of(kj * tq, tq), tq)
                k = k_ref[rows, sl]
                v = v_ref[rows, sl]
                s = _dot_nt(qm, k)
                c = mask_ref[qi - kj]
                sm = jnp.where(c > 0.0, s, NEG_BIG)
                m_new = jnp.maximum(m, jnp.max(sm, axis=1, keepdims=True))
                alpha = jnp.exp(m - m_new)
                p = jnp.exp(sm - m_new) * c
                l = alpha * l + jnp.sum(p, axis=1, keepdims=True)
                acc = alpha * acc + _dot(p, v)
                return m_new, l, acc

            init = (jnp.full((tq, 1), NEG_BIG, F32), jnp.zeros((tq, 1), F32), jnp.zeros((tq, LANES), F32))
            _, l, acc = lax.fori_loop(0, qi + 1, body, init)
            outs.append(acc / l)
        o_ref[:, sl] = jnp.where(first, outs[0], outs[1])


def _attn_prompt(q, k, qkv, batch, seq, tq=256):
    nb = seq // tq
    pos = np.arange(tq)
    masks = np.stack([_multiplicity(pos + d * tq, pos) for d in range(nb)])
    return pl.pallas_call(
        functools.partial(_attn_prompt_kernel, tq=tq),
        grid=(batch, nb),
        in_specs=[pl.BlockSpec((tq, D_ATT), lambda b, i: (b * nb + i, 0)),
                  pl.BlockSpec((seq, D_ATT), lambda b, i: (b, 0)),
                  pl.BlockSpec((seq, D_ATT), lambda b, i: (b, 2)),
                  pl.BlockSpec((nb, tq, tq), lambda b, i: (0, 0, 0))],
        out_specs=pl.BlockSpec((tq, D_ATT), lambda b, i: (b * nb + i, 0)),
        out_shape=jax.ShapeDtypeStruct((batch * seq, D_ATT), F32),
        compiler_params=_cparams(("parallel", "arbitrary")),
        name="attn_prompt",
    )(q, k, qkv, jnp.asarray(masks))


def _attn_sample_kernel(q_ref, kn_ref, vn_ref, ka_ref, va_ref, kb_ref, vb_ref, ma_ref, mb_ref, mc_ref, o_ref,
                        *, bb, n_res):
    lane = lax.broadcasted_iota(jnp.int32, (1, LANES), 1)
    first = lane < HEAD_DIM
    mb = mb_ref[...]
    mc = mc_ref[...]
    for b in range(bb):
        for hp in range(HEAD_PAIRS):
            sl = slice(hp * LANES, (hp + 1) * LANES)
            qp = q_ref[b, :, sl]
            tq = qp.shape[0]
            qs = jnp.concatenate([jnp.where(first, qp, 0.0), jnp.where(first, 0.0, qp)], axis=0).astype(BF16)
            parts = []
            for r in range(n_res):
                cs = slice(r * D_ATT + hp * LANES, r * D_ATT + (hp + 1) * LANES)
                parts.append((_dot_nt(qs, ka_ref[b, :, cs]), ma_ref[r], va_ref[b, :, cs]))
            parts.append((_dot_nt(qs, kb_ref[b, :, sl]), mb, vb_ref[b, :, sl]))
            parts.append((_dot_nt(qs, kn_ref[b, :, sl]), mc, vn_ref[b, :, sl]))
            m = jnp.full((2 * tq, 1), NEG_BIG, F32)
            masked = []
            for s, c, _ in parts:
                sm = jnp.where(c > 0.0, s, NEG_BIG)
                masked.append(sm)
                m = jnp.maximum(m, jnp.max(sm, axis=1, keepdims=True))
            l = jnp.zeros((2 * tq, 1), F32)
            acc = jnp.zeros((2 * tq, LANES), F32)
            for sm, (_, c, v) in zip(masked, parts):
                p = jnp.exp(sm - m) * c
                l = l + jnp.sum(p, axis=1, keepdims=True)
                acc = acc + _dot(p, v)
            o = acc / l
            o_ref[b, :, sl] = jnp.where(first, o[:tq], o[tq:])


def _attn_sample(q, k, qkv, row0, cache_k, cache_v, layer, batch, tq, past, bb=2):
    dil = PATTERNS[-1][1]
    near = PATTERNS[-2][0]
    split = past - near
    groups = past // dil
    ga = split // dil
    q_pos = past + np.arange(tq)
    full = _multiplicity(q_pos, np.arange(past))
    dropped = np.ones(past, bool)
    dropped[split:] = False
    dropped[(np.arange(past) % dil) < tq] = False
    assert not full[:, dropped].any()
    two = lambda c: np.concatenate([c, c], axis=0)
    ma = np.stack([two(_multiplicity(q_pos, dil * np.arange(ga) + r)) for r in range(tq)])
    mb = two(_multiplicity(q_pos, np.arange(split, past)))
    mc = two(_multiplicity(q_pos, q_pos))
    q3 = q.reshape(batch, tq, D_ATT)
    k3 = k.reshape(batch, tq, D_ATT)
    rb0 = row0 // tq
    v3 = qkv.reshape(qkv.shape[0] // tq, tq, 3 * D_ATT)
    depth = cache_k.shape[0]
    ck_a = cache_k.reshape(depth, batch, groups, dil * D_ATT)
    cv_a = cache_v.reshape(depth, batch, groups, dil * D_ATT)
    ck_b = cache_k.reshape(depth, batch, past, D_ATT)
    cv_b = cache_v.reshape(depth, batch, past, D_ATT)
    a_spec = pl.BlockSpec((None, bb, ga, tq * D_ATT), lambda i: (layer, i, 0, 0))
    b_spec = pl.BlockSpec((None, bb, near, D_ATT), lambda i: (layer, i, split // near, 0))
    new_spec = pl.BlockSpec((bb, tq, D_ATT), lambda i: (i, 0, 0))
    full_spec = lambda a: pl.BlockSpec(a.shape, lambda i: (0,) * a.ndim)
    ma, mb, mc = jnp.asarray(ma), jnp.asarray(mb), jnp.asarray(mc)
    return pl.pallas_call(
        functools.partial(_attn_sample_kernel, bb=bb, n_res=tq),
        grid=(batch // bb,),
        in_specs=[new_spec, new_spec,
                  pl.BlockSpec((bb, tq, D_ATT), lambda i: (rb0 // bb + i, 0, 2)),
                  a_spec, a_spec, b_spec, b_spec, full_spec(ma), full_spec(mb), full_spec(mc)],
        out_specs=new_spec,
        out_shape=jax.ShapeDtypeStruct((batch, tq, D_ATT), F32),
        compiler_params=_cparams(("parallel",)),
        name="attn_sample",
    )(q3, k3, v3, ck_a, cv_a, ck_b, cv_b, ma, mb, mc)


def _rglru_kernel(x_ref, g_ref, tail0_ref, h0_ref, cw_ref, cb_ref, wa_ref, ba_ref, wx_ref, bx_ref, lam_ref,
                  y_ref, tail_ref, hl_ref, xpad, a_s, b_s, hc, *, batch, tt):
    rows = tt * batch
    pad = (CONV_W - 1) * batch
    i = pl.program_id(0)

    @pl.when(i == 0)
    def _():
        xpad[0:pad, :] = tail0_ref[...]
        hc[...] = h0_ref[...]

    @pl.when(i > 0)
    def _():
        xpad[0:pad, :] = xpad[rows:rows + pad, :]

    xpad[pad:pad + rows, :] = x_ref[...]
    xc = cb_ref[...] + cw_ref[0:1, :] * xpad[0:rows, :]
    for j in range(1, CONV_W):
        xc = xc + cw_ref[j:j + 1, :] * xpad[j * batch:j * batch + rows, :]
    r = jax.nn.sigmoid(_dot(xc, wa_ref[...]) + ba_ref[...])
    ig = jax.nn.sigmoid(_dot(xc, wx_ref[...]) + bx_ref[...])
    log_a = (-LRU_C) * r * _softplus(-lam_ref[...])
    a = jnp.exp(log_a)
    a_s[...] = a
    b_s[...] = jnp.sqrt(-jnp.tanh(log_a) * (a * a + 1.0)) * (ig * xc)

    def step(t, h):
        sl = pl.ds(pl.multiple_of(t * batch, batch), batch)
        h = a_s[sl, :] * h + b_s[sl, :]
        a_s[sl, :] = h
        return h

    h = lax.fori_loop(0, tt, step, hc[...])
    hc[...] = h
    y_ref[...] = a_s[...] * jax.nn.gelu(g_ref[...])
    tail_ref[...] = xpad[rows:rows + pad, :]
    hl_ref[...] = h


def _block_diag(w):
    nb, n, _ = w.shape
    eye = jnp.eye(nb, dtype=w.dtype)
    return (w[:, :, None, :] * eye[:, None, :, None]).reshape(nb * n, nb * n)


def _rglru(x_tm, g_tm, tail0, h0, p, batch, seq, tt):
    rows = tt * batch
    pad = (CONV_W - 1) * batch
    row_spec = pl.BlockSpec((rows, D_ATT), lambda i: (i, 0))
    const = lambda shape: pl.BlockSpec(shape, lambda i: (0,) * len(shape))
    vec = const((1, D_ATT))
    return pl.pallas_call(
        functools.partial(_rglru_kernel, batch=batch, tt=tt),
        grid=(seq // tt,),
        in_specs=[row_spec, row_spec, const((pad, D_ATT)), const((batch, D_ATT)), const((CONV_W, D_ATT)), vec,
                  const((D_ATT, D_ATT)), vec, const((D_ATT, D_ATT)), vec, vec],
        out_specs=[row_spec, const((pad, D_ATT)), const((batch, D_ATT))],
        out_shape=[jax.ShapeDtypeStruct((seq * batch, D_ATT), F32),
                   jax.ShapeDtypeStruct((pad, D_ATT), F32),
                   jax.ShapeDtypeStruct((batch, D_ATT), F32)],
        scratch_shapes=[pltpu.VMEM((pad + rows, D_ATT), F32), pltpu.VMEM((rows, D_ATT), F32),
                        pltpu.VMEM((rows, D_ATT), F32), pltpu.VMEM((batch, D_ATT), F32)],
        compiler_params=_cparams(("arbitrary",)),
        name="rglru",
    )(x_tm, g_tm, tail0, h0, p["conv_w"], p["conv_b"].reshape(1, D_ATT),
      _block_diag(p["rg_wa"]).astype(BF16), p["rg_ba"].reshape(1, D_ATT),
      _block_diag(p["rg_wx"]).astype(BF16), p["rg_bx"].reshape(1, D_ATT), p["rg_lambda"].reshape(1, D_ATT))


def _rwkv_pre_kernel(c_ref, prev_ref, mu_ref, w0_ref, w2_ref, a0_ref, a2_ref, g2_ref, kk_ref, ka_ref, ones_ref,
                     r_ref, k_ref, v_ref, kap_ref, a_ref, lw_ref, g_ref):
    c = c_ref[...]
    cm = c + (prev_ref[...] - c) * mu_ref[...]
    r = cm[:, 0:D_ATT]
    k = cm[:, D_ATT:2 * D_ATT]
    v = cm[:, 2 * D_ATT:3 * D_ATT]
    lo = cm[:, 3 * D_ATT:3 * D_ATT + LORA_W + LORA_A]
    g_lo = cm[:, 3 * D_ATT + LORA_W + LORA_A:]
    w = -_softplus(-(w0_ref[...] + _dot(jnp.tanh(lo), w2_ref[...]))) - 0.5
    a = jax.nn.sigmoid(a0_ref[...] + _dot(lo, a2_ref[...]))
    g = _dot(jax.nn.sigmoid(g_lo), g2_ref[...])
    kk = k * kk_ref[...]
    nrm = jnp.sqrt(_dot_split(kk * kk, ones_ref[...], 2))
    r_ref[...] = r
    k_ref[...] = k * (1.0 + (a - 1.0) * ka_ref[...])
    v_ref[...] = v
    kap_ref[...] = kk / jnp.maximum(nrm, 1e-12)
    a_ref[...] = a
    lw_ref[...] = -jnp.exp(w)
    g_ref[...] = g


def _rwkv_pre(cc, prev, p, tm=256):
    n = cc.shape[0]
    lo = LORA_W + LORA_A
    w2p = jnp.concatenate([p["rwkv_w2"], jnp.zeros((LORA_A, D_ATT), F32)], axis=0).astype(BF16)
    a2p = jnp.concatenate([jnp.zeros((LORA_W, D_ATT), F32), p["rwkv_a2"]], axis=0).astype(BF16)
    row = pl.BlockSpec((tm, C_COLS), lambda i: (i, 0))
    const = lambda shape: pl.BlockSpec(shape, lambda i: (0,) * len(shape))
    vec = const((1, D_ATT))
    out = pl.BlockSpec((tm, D_ATT), lambda i: (i, 0))
    return pl.pallas_call(
        _rwkv_pre_kernel,
        grid=(n // tm,),
        in_specs=[row, row, const((1, C_COLS)), vec, const((lo, D_ATT)), vec, const((lo, D_ATT)),
                  const((LORA_G, D_ATT)), vec, vec, const((D_ATT, D_ATT))],
        out_specs=[out] * 7,
        out_shape=[jax.ShapeDtypeStruct((n, D_ATT), F32)] * 7,
        compiler_params=_cparams(("parallel",)),
        name="rwkv_pre",
    )(cc, prev, p["rwkv_mu"].reshape(1, C_COLS), p["rwkv_w0"].reshape(1, D_ATT), w2p,
      p["rwkv_a0"].reshape(1, D_ATT), a2p, p["rwkv_g2"].astype(BF16), p["rwkv_kk"].reshape(1, D_ATT),
      p["rwkv_ka"].reshape(1, D_ATT), _head_ones())


def _wkv_kernel(r_ref, k_ref, v_ref, kap_ref, a_ref, lw_ref, g_ref, s0_ref, rk_ref, lnw_ref, lnb_ref,
                tri_ref, sl_ref, il_ref, eye_ref, bm_ref, ones_ref, o_ref, sl_out_ref, s_scr, y_scr, *, c):
    ci = pl.program_id(1)

    @pl.when(ci == 0)
    def _():
        s_scr[...] = s0_ref[0]

    r, k, v, kap, a, lw = r_ref[0], k_ref[0], v_ref[0], kap_ref[0], a_ref[0], lw_ref[0]
    cum = _split_dot_left(tri_ref[...], lw, 3)
    tot = cum[c - 1:c, :]
    e_pos, e_neg, e_end = jnp.exp(cum), jnp.exp(-cum), jnp.exp(tot - cum)
    b = kap * a
    kap_t = kap * jnp.exp(cum - lw)
    r_t = r * e_pos
    k_t = k * e_neg
    b_t = b * e_neg
    k_h = k * e_end
    b_h = b * e_end
    g_end = jnp.exp(tot)

    lane = lax.broadcasted_iota(jnp.int32, (1, LANES), 1)
    first = lane < HEAD_DIM
    stack = lambda z: jnp.concatenate([jnp.where(first, z, 0.0), jnp.where(first, 0.0, z)], axis=0)
    twice = lambda z: jnp.concatenate([z, z], axis=0)
    fold = lambda z: z[:c] + z[c:]
    strict, incl, eye, bmask = sl_ref[...], il_ref[...], eye_ref[...], bm_ref[...]
    n_fac = max(1, int(np.ceil(np.log2(c))))
    pad_rows = (-2 * c) % LANES

    for hp in range(HEAD_PAIRS):
        sl = slice(hp * LANES, (hp + 1) * LANES)
        s_p = s_scr[hp]
        kap_s, r_s = stack(kap_t[:, sl]), stack(r_t[:, sl])
        k_2, b_2 = twice(k_t[:, sl]), twice(b_t[:, sl])
        v_s = stack(v[:, sl])
        a_kk = _dot_nt(kap_s, k_2) * strict
        a_bk = _dot_nt(kap_s, b_2) * strict
        b_kr = _dot_nt(r_s, k_2) * incl
        b_br = _dot_nt(r_s, b_2) * incl
        pw = -a_bk
        inv = eye + pw
        for _ in range(n_fac - 1):
            pw = _dot(pw, pw)
            inv = inv + _dot(inv, pw)
        x_s = stack(_dot_nt(kap_t[:, sl], s_p)) + _dot(a_kk, v_s)
        u_s = _dot(inv, x_s)
        y_s = stack(_dot_nt(r_t[:, sl], s_p)) + _dot(b_kr, v_s) - _dot(b_br, u_s)
        y_scr[:, sl] = fold(y_s)
        z = jnp.concatenate([v[:, sl], fold(u_s)], axis=0)
        w = jnp.concatenate([k_h[:, sl], -b_h[:, sl]], axis=0)
        if pad_rows:
            zeros = jnp.zeros((pad_rows, LANES), F32)
            z = jnp.concatenate([z, zeros], axis=0)
            w = jnp.concatenate([w, zeros], axis=0)
        s_scr[hp] = s_p * g_end[:, sl] + bmask * _dot(z.T, w)

    y = y_scr[...]
    ones = ones_ref[...]
    inv_n = 1.0 / HEAD_DIM
    mean = _dot_split(y, ones, 3) * inv_n
    d = y - mean
    var = _dot_split(d * d, ones, 3) * inv_n
    yn = d * lax.rsqrt(var + GN_EPS) * lnw_ref[...] + lnb_ref[...]
    bonus = _dot_split(r * k * rk_ref[...], ones, 3) * v
    o_ref[0] = (yn + bonus) * g_ref[0]

    @pl.when(ci == pl.num_programs(1) - 1)
    def _():
        sl_out_ref[0] = s_scr[...]


def _wkv(parts, row0, s0_pairs, p, batch, seq, c):
    nc = seq // c
    rb0 = row0 // c
    n3 = [z.reshape(z.shape[0] // c, c, D_ATT) for z in parts]
    idx = np.arange(2 * c)
    same = (idx[:, None] // c) == (idx[None, :] // c)
    strict = (same & (idx[None, :] % c < idx[:, None] % c)).astype(np.float32)
    incl = (same & (idx[None, :] % c <= idx[:, None] % c)).astype(np.float32)
    eye = np.eye(2 * c, dtype=np.float32)
    half = np.arange(LANES) // HEAD_DIM
    bmask = (half[:, None] == half[None, :]).astype(np.float32)
    tri = (np.arange(c)[None, :] <= np.arange(c)[:, None]).astype(np.float32)
    chunk = pl.BlockSpec((1, c, D_ATT), lambda b, i: (rb0 + b * nc + i, 0, 0))
    const = lambda shape: pl.BlockSpec(shape, lambda b, i: (0,) * len(shape))
    vec = const((1, D_ATT))
    state = pl.BlockSpec((1, HEAD_PAIRS, LANES, LANES), lambda b, i: (b, 0, 0, 0))
    return pl.pallas_call(
        functools.partial(_wkv_kernel, c=c),
        grid=(batch, nc),
        in_specs=[chunk] * 7 + [state, vec, vec, vec, const((c, c)), const((2 * c, 2 * c)), const((2 * c, 2 * c)),
                                const((2 * c, 2 * c)), const((LANES, LANES)), const((D_ATT, D_ATT))],
        out_specs=[pl.BlockSpec((1, c, D_ATT), lambda b, i: (b * nc + i, 0, 0)), state],
        out_shape=[jax.ShapeDtypeStruct((batch * nc, c, D_ATT), F32),
                   jax.ShapeDtypeStruct((batch, HEAD_PAIRS, LANES, LANES), F32)],
        scratch_shapes=[pltpu.VMEM((HEAD_PAIRS, LANES, LANES), F32), pltpu.VMEM((c, D_ATT), F32)],
        compiler_params=_cparams(("parallel", "arbitrary")),
        name="wkv",
    )(*n3, s0_pairs, p["rwkv_rk"].reshape(1, D_ATT), p["rwkv_lnw"].reshape(1, D_ATT),
      p["rwkv_lnb"].reshape(1, D_ATT), jnp.asarray(tri, dtype=BF16), jnp.asarray(strict), jnp.asarray(incl),
      jnp.asarray(eye), jnp.asarray(bmask), _head_ones())


def _pair_states(s):
    b = s.shape[0]
    s = s.reshape(b, HEAD_PAIRS, 2, HEAD_DIM, HEAD_DIM)
    z = jnp.zeros((b, HEAD_PAIRS, HEAD_DIM, HEAD_DIM), s.dtype)
    top = jnp.concatenate([s[:, :, 0], z], axis=-1)
    bot = jnp.concatenate([z, s[:, :, 1]], axis=-1)
    return jnp.concatenate([top, bot], axis=-2)


def _unpair_states(sp):
    b = sp.shape[0]
    first = sp[:, :, :HEAD_DIM, :HEAD_DIM]
    second = sp[:, :, HEAD_DIM:, HEAD_DIM:]
    return jnp.stack([first, second], axis=2).reshape(b, N_HEADS, HEAD_DIM, HEAD_DIM)


def _merge_kernel(oa_ref, ob_ref, oc_ref, gl_ref, x_ref, bm_ref, wa_ref, wb_ref, wc_ref, wo_ref, nw_ref,
                  wgh_ref, wgl_ref, bg_ref, x1_ref, xn_ref, comb_ref):
    merged = None
    for j, (o_ref, w_ref) in enumerate(((oa_ref, wa_ref), (ob_ref, wb_ref), (oc_ref, wc_ref))):
        cols = slice(j * D_MODEL, (j + 1) * D_MODEL)
        gate = jax.nn.sigmoid(gl_ref[:, cols] + bm_ref[:, cols])
        t = gate * _dot(o_ref[...], w_ref[...])
        merged = t if merged is None else merged + t
    x1 = x_ref[...] + _dot(merged, wo_ref[...])
    x1_ref[...] = x1
    xn = x1 * lax.rsqrt(jnp.mean(x1 * x1, axis=-1, keepdims=True) + EPS) * nw_ref[...]
    xn_ref[...] = xn.astype(BF16)
    hi = xn.astype(BF16)
    lo = (xn - hi.astype(F32)).astype(BF16)
    wgh, wgl = wgh_ref[...], wgl_ref[...]
    lg = (jnp.dot(hi, wgh, preferred_element_type=F32) + jnp.dot(lo, wgh, preferred_element_type=F32)
          + jnp.dot(hi, wgl, preferred_element_type=F32)) + bg_ref[...]
    col = lambda j: lg[:, j:j + 1]
    best = col(0)
    gi = jnp.zeros(best.shape, jnp.int32)
    for j in range(1, N_GROUPS):
        better = col(j) > best
        best = jnp.where(better, col(j), best)
        gi = jnp.where(better, j, gi)
    den = jnp.zeros(best.shape, F32)
    for j in range(N_GROUPS):
        den = den + jnp.exp(col(j) - best)
    p_group = 1.0 / den
    le = []
    for j in range(EXP_PER_GROUP):
        z = col(N_GROUPS + j)
        for grp in range(1, N_GROUPS):
            z = jnp.where(gi == grp, col(N_GROUPS + grp * EXP_PER_GROUP + j), z)
        le.append(z)
    v1 = le[0]
    i1 = jnp.zeros(best.shape, jnp.int32)
    for j in range(1, EXP_PER_GROUP):
        better = le[j] > v1
        v1 = jnp.where(better, le[j], v1)
        i1 = jnp.where(better, j, i1)
    v2 = jnp.full(best.shape, -jnp.inf, F32)
    i2 = jnp.zeros(best.shape, jnp.int32)
    for j in range(EXP_PER_GROUP):
        cand = jnp.where(i1 == j, -jnp.inf, le[j])
        better = cand > v2
        v2 = jnp.where(better, cand, v2)
        i2 = jnp.where(better, j, i2)
    e2 = jnp.exp(v2 - v1)
    w1 = p_group / (1.0 + e2)
    w2 = p_group * e2 / (1.0 + e2)
    lane = lax.broadcasted_iota(jnp.int32, lg.shape, 1)
    base = gi * EXP_PER_GROUP
    comb_ref[...] = jnp.where(lane == base + i1, w1, 0.0) + jnp.where(lane == base + i2, w2, 0.0)


def _merge(oa, ob, oc, gl, x, p, tm=256):
    n = x.shape[0]
    wg = jnp.concatenate([p["moe_wg_group"], p["moe_wg_exp"]], axis=1)
    wg = jnp.pad(wg, ((0, 0), (0, LANES - wg.shape[1])))
    wgh = wg.astype(BF16)
    wgl = (wg - wgh.astype(F32)).astype(BF16)
    bg = jnp.pad(jnp.concatenate([p["moe_bg_group"], p["moe_bg_exp"]]), (0, LANES - N_GROUPS - N_EXPERTS))
    row = lambda w: pl.BlockSpec((tm, w), lambda i: (i, 0))
    const = lambda shape: pl.BlockSpec(shape, lambda i: (0,) * len(shape))
    return pl.pallas_call(
        _merge_kernel,
        grid=(n // tm,),
        in_specs=[row(D_ATT), row(D_ATT), row(D_ATT), row(GL_COLS), row(D_MODEL), const((1, GL_COLS)),
                  const((D_ATT, D_MODEL)), const((D_ATT, D_MODEL)), const((D_ATT, D_MODEL)),
                  const((D_MODEL, D_MODEL)), const((1, D_MODEL)), const((D_MODEL, LANES)),
                  const((D_MODEL, LANES)), const((1, LANES))],
        out_specs=[row(D_MODEL), row(D_MODEL), row(LANES)],
        out_shape=[jax.ShapeDtypeStruct((n, D_MODEL), F32), jax.ShapeDtypeStruct((n, D_MODEL), BF16),
                   jax.ShapeDtypeStruct((n, LANES), F32)],
        compiler_params=_cparams(("parallel",)),
        name="merge",
    )(oa, ob, oc, gl, x, p["b_merge"].reshape(1, GL_COLS), p["w_branch_a"].astype(BF16),
      p["w_branch_b"].astype(BF16), p["w_branch_c"].astype(BF16), p["w_out"].astype(BF16),
      p["norm2_w"].reshape(1, D_MODEL), wgh, wgl, bg.reshape(1, LANES))


def _moe_kernel(xn_ref, comb_ref, x1_ref, w1_ref, w3_ref, w2_ref, o_ref, acc):
    e = pl.program_id(1)

    @pl.when(e == 0)
    def _():
        acc[...] = x1_ref[...]

    xb = xn_ref[...]
    lane = lax.broadcasted_iota(jnp.int32, comb_ref.shape, 1)
    wt = jnp.sum(jnp.where(lane == e, comb_ref[...], 0.0), axis=1, keepdims=True)
    h = jax.nn.silu(jnp.dot(xb, w1_ref[0], preferred_element_type=F32)) * jnp.dot(xb, w3_ref[0],
                                                                                  preferred_element_type=F32)
    acc[...] += _dot(h * wt, w2_ref[0])

    @pl.when(e == pl.num_programs(1) - 1)
    def _():
        o_ref[...] = acc[...]


def _moe(xn, comb, x1, w1, w3, w2, tm=1024):
    n = xn.shape[0]
    return pl.pallas_call(
        _moe_kernel,
        grid=(n // tm, N_EXPERTS),
        in_specs=[pl.BlockSpec((tm, D_MODEL), lambda i, e: (i, 0)),
                  pl.BlockSpec((tm, LANES), lambda i, e: (i, 0)),
                  pl.BlockSpec((tm, D_MODEL), lambda i, e: (i, 0)),
                  pl.BlockSpec((1, D_MODEL, D_EXPERT), lambda i, e: (e, 0, 0)),
                  pl.BlockSpec((1, D_MODEL, D_EXPERT), lambda i, e: (e, 0, 0)),
                  pl.BlockSpec((1, D_EXPERT, D_MODEL), lambda i, e: (e, 0, 0))],
        out_specs=pl.BlockSpec((tm, D_MODEL), lambda i, e: (i, 0)),
        out_shape=jax.ShapeDtypeStruct((n, D_MODEL), F32),
        scratch_shapes=[pltpu.VMEM((tm, D_MODEL), F32)],
        compiler_params=_cparams(("parallel", "arbitrary")),
        name="moe",
    )(xn, comb, x1, w1, w3, w2)


def _time_major(z, batch, seq):
    return z.reshape(batch, seq, z.shape[-1]).transpose(1, 0, 2).reshape(seq * batch, z.shape[-1])


def _batch_major(z, batch, seq):
    return z.reshape(seq, batch, z.shape[-1]).transpose(1, 0, 2).reshape(batch * seq, z.shape[-1])


def _layer(x, groups, p, caches, layer, tabs):
    qkv, rg, cc, gl = _inproj(x, p["norm1_w"], p["w_in"].astype(BF16))
    n = x.shape[0]

    prev_parts = []
    for g in groups:
        c3 = cc[g["row0"]:g["row0"] + g["batch"] * g["seq"]].reshape(g["batch"], g["seq"], C_COLS)
        prev_parts.append(jnp.concatenate([g["shift"][:, None, :], c3[:, :-1]], axis=1).reshape(-1, C_COLS))
    prev = jnp.concatenate(prev_parts, axis=0)
    wkv_parts = _rwkv_pre(cc, prev, p)

    oa, ob, oc, new_states = [], [], [], []
    for gi, g in enumerate(groups):
        row0, batch, seq = g["row0"], g["batch"], g["seq"]
        rows = batch * seq
        tm = 256
        q, k = _qkprep(qkv, row0, rows, tabs[gi], g["table_blocks"], p["q_norm_w"], p["k_norm_w"], tm)
        if g["past"] == 0:
            o_att = _attn_prompt(q, k, qkv, batch, seq)
        else:
            o_att = _attn_sample(q, k, qkv, row0, caches[0], caches[1], layer, batch, seq, g["past"])
        oa.append(o_att.reshape(rows, D_ATT))
        v_rows = qkv[row0:row0 + rows, 2 * D_ATT:]
        xr = _time_major(rg[row0:row0 + rows, :D_ATT], batch, seq)
        gr = _time_major(rg[row0:row0 + rows, D_ATT:], batch, seq)
        tail0 = g["conv"].transpose(1, 0, 2).reshape((CONV_W - 1) * batch, D_ATT)
        y_tm, tail, h_last = _rglru(xr, gr, tail0, g["h"], p, batch, seq, g["tt"])
        ob.append(_batch_major(y_tm, batch, seq))
        conv_tail = tail.reshape(CONV_W - 1, batch, D_ATT).transpose(1, 0, 2)
        o_wkv, s_last = _wkv(wkv_parts, row0, _pair_states(g["wkv"]), p, batch, seq, g["chunk"])
        oc.append(o_wkv.reshape(rows, D_ATT))
        shift_last = cc[row0:row0 + rows].reshape(batch, seq, C_COLS)[:, -1]
        new_states.append((k.reshape(batch, seq, N_HEADS, HEAD_DIM), v_rows.reshape(batch, seq, N_HEADS, HEAD_DIM),
                           conv_tail, h_last, shift_last, _unpair_states(s_last)))

    cat = lambda zs: jnp.concatenate(zs, axis=0)
    x1, xn2, comb = _merge(cat(oa), cat(ob), cat(oc), gl, x, p)
    x2 = _moe(xn2, comb, x1, p["moe_w1"].astype(BF16), p["moe_w3"].astype(BF16), p["moe_w2"].astype(BF16))
    return x2, new_states


def kernel(x_prompt, x_sample, cache_k_win, cache_v_win, state_conv, state_h, state_shift, state_wkv, norm1_w, w_in, q_norm_w, k_norm_w, conv_w, conv_b, rg_wa, rg_ba, rg_wx, rg_bx, rg_lambda, rwkv_mu, rwkv_w0, rwkv_w2, rwkv_a0, rwkv_a2, rwkv_g2, rwkv_kk, rwkv_ka, rwkv_rk, rwkv_lnw, rwkv_lnb, b_merge, w_branch_a, w_branch_b, w_branch_c, w_out, norm2_w, moe_wg_group, moe_bg_group, moe_wg_exp, moe_bg_exp, moe_w1, moe_w3, moe_w2):
    weights = dict(norm1_w=norm1_w, w_in=w_in, q_norm_w=q_norm_w, k_norm_w=k_norm_w, conv_w=conv_w, conv_b=conv_b,
                   rg_wa=rg_wa, rg_ba=rg_ba, rg_wx=rg_wx, rg_bx=rg_bx, rg_lambda=rg_lambda, rwkv_mu=rwkv_mu,
                   rwkv_w0=rwkv_w0, rwkv_w2=rwkv_w2, rwkv_a0=rwkv_a0, rwkv_a2=rwkv_a2, rwkv_g2=rwkv_g2,
                   rwkv_kk=rwkv_kk, rwkv_ka=rwkv_ka, rwkv_rk=rwkv_rk, rwkv_lnw=rwkv_lnw, rwkv_lnb=rwkv_lnb,
                   b_merge=b_merge, w_branch_a=w_branch_a, w_branch_b=w_branch_b, w_branch_c=w_branch_c,
                   w_out=w_out, norm2_w=norm2_w, moe_wg_group=moe_wg_group, moe_bg_group=moe_bg_group,
                   moe_wg_exp=moe_wg_exp, moe_bg_exp=moe_bg_exp, moe_w1=moe_w1, moe_w3=moe_w3, moe_w2=moe_w2)
    depth = w_in.shape[0]
    bp, tp = x_prompt.shape[0], x_prompt.shape[1]
    bs, ts = x_sample.shape[0], x_sample.shape[1]
    past = cache_k_win.shape[2]
    np_rows = bp * tp
    x = jnp.concatenate([x_prompt.reshape(np_rows, D_MODEL), x_sample.reshape(bs * ts, D_MODEL)], axis=0)
    tm = 256
    tab_p = _rope_tables(jnp.arange(tp, dtype=jnp.int32))
    tab_s = tuple(jnp.tile(t, (tm // ts, 1)) for t in _rope_tables(past + jnp.arange(ts, dtype=jnp.int32)))
    outs_p = [[] for _ in range(6)]
    outs_s = [[] for _ in range(6)]
    for l in range(depth):
        p = {k_: v_[l] for k_, v_ in weights.items()}
        groups = [
            dict(row0=0, batch=bp, seq=tp, past=0, table_blocks=tp // tm, tt=64, chunk=64,
                 conv=jnp.zeros((bp, CONV_W - 1, D_ATT), F32), h=jnp.zeros((bp, D_ATT), F32),
                 shift=jnp.zeros((bp, C_COLS), F32), wkv=jnp.zeros((bp, N_HEADS, HEAD_DIM, HEAD_DIM), F32)),
            dict(row0=np_rows, batch=bs, seq=ts, past=past, table_blocks=1, tt=ts, chunk=ts,
                 conv=state_conv[l], h=state_h[l], shift=state_shift[l], wkv=state_wkv[l]),
        ]
        x, (st_p, st_s) = _layer(x, groups, p, (cache_k_win, cache_v_win), l, (tab_p, tab_s))
        for j in range(6):
            outs_p[j].append(st_p[j])
            outs_s[j].append(st_s[j])
    y_prompt = x[:np_rows].reshape(bp, tp, D_MODEL)
    y_sample = x[np_rows:].reshape(bs, ts, D_MODEL)
    stack = lambda zs: jnp.stack(zs)
    return (y_prompt, y_sample, *[stack(z) for z in outs_p], *[stack(z) for z in outs_s])
```

```python
import functools

import numpy as np
import jax
import jax.numpy as jnp
from jax import lax
from jax.experimental import pallas as pl
from jax.experimental.pallas import tpu as pltpu

F32 = jnp.float32
BF16 = jnp.bfloat16

D_MODEL = 1024
N_HEADS = 8
HEAD_DIM = 64
D_ATT = N_HEADS * HEAD_DIM
ROT_DIM = HEAD_DIM // 4
ROPE_THETA = 500000.0
PATTERNS = ((128, 1), (512, 4), (2048, 16))
CONV_W = 4
LRU_C = 8.0
LORA_W, LORA_A, LORA_G = 64, 64, 128
C_COLS = 3 * D_ATT + LORA_W + LORA_A + LORA_G
N_BRANCH = 3
GL_COLS = N_BRANCH * D_MODEL
N_GROUPS, EXP_PER_GROUP, N_EXPERTS = 4, 4, 16
D_EXPERT = 512
EPS = 1e-6
GN_EPS = 64e-5

LANES = 128
SUBLANES = 8
VMEM_LIMIT = 56 * 1024 * 1024
HEAD_PAIRS = D_ATT // LANES
NEG_BIG = -1e30


def _cparams(sem):
    return pltpu.CompilerParams(dimension_semantics=sem, vmem_limit_bytes=VMEM_LIMIT)


def _dot(a, b):
    return jnp.dot(a.astype(BF16), b.astype(BF16), preferred_element_type=F32)


def _dot_nt(a, b):
    return lax.dot_general(a.astype(BF16), b.astype(BF16), (((1,), (1,)), ((), ())),
                           preferred_element_type=F32)


def _dot_split(x, w_bf16, passes):
    acc = None
    r = x
    for i in range(passes):
        p = r.astype(BF16)
        t = jnp.dot(p, w_bf16, preferred_element_type=F32)
        acc = t if acc is None else acc + t
        if i + 1 < passes:
            r = r - p.astype(F32)
    return acc


def _split_dot_left(w_exact_bf16, x, passes):
    acc = None
    r = x
    for i in range(passes):
        p = r.astype(BF16)
        t = jnp.dot(w_exact_bf16, p, preferred_element_type=F32)
        acc = t if acc is None else acc + t
        if i + 1 < passes:
            r = r - p.astype(F32)
    return acc


def _softplus(x):
    return jnp.maximum(x, 0.0) + jnp.log1p(jnp.exp(-jnp.abs(x)))


def _head_ones():
    h = np.arange(D_ATT) // HEAD_DIM
    return jnp.asarray((h[:, None] == h[None, :]).astype(np.float32), dtype=BF16)


def _inproj_kernel(x_ref, nw_ref, w_ref, qkv_ref, rg_ref, cc_ref, gl_ref):
    x = x_ref[...]
    xn = x * lax.rsqrt(jnp.mean(x * x, axis=-1, keepdims=True) + EPS) * nw_ref[...]
    xb = xn.astype(BF16)
    off = 0
    for ref in (qkv_ref, rg_ref, cc_ref, gl_ref):
        n = ref.shape[-1]
        ref[...] = jnp.dot(xb, w_ref[:, off:off + n], preferred_element_type=F32)
        off += n


def _inproj(x, norm_w, w_in_bf16, tm=256):
    n = x.shape[0]
    widths = (3 * D_ATT, 2 * D_ATT, C_COLS, GL_COLS)
    in_cols = sum(widths)
    return pl.pallas_call(
        _inproj_kernel,
        grid=(n // tm,),
        in_specs=[pl.BlockSpec((tm, D_MODEL), lambda i: (i, 0)),
                  pl.BlockSpec((1, D_MODEL), lambda i: (0, 0)),
                  pl.BlockSpec((D_MODEL, in_cols), lambda i: (0, 0))],
        out_specs=[pl.BlockSpec((tm, w), lambda i: (i, 0)) for w in widths],
        out_shape=[jax.ShapeDtypeStruct((n, w), F32) for w in widths],
        compiler_params=_cparams(("parallel",)),
        name="inproj",
    )(x, norm_w.reshape(1, D_MODEL), w_in_bf16)


def _rope_tables(pos):
    half = ROT_DIM // 2
    inv = ROPE_THETA ** (-jnp.arange(half, dtype=F32) / half)
    ang = pos.astype(F32)[:, None] * inv[None, :]
    cos, sin = jnp.cos(ang), jnp.sin(ang)
    t = pos.shape[0]
    rest = HEAD_DIM - ROT_DIM
    c_head = jnp.concatenate([cos, cos, jnp.ones((t, rest), F32)], axis=1)
    s_prev = jnp.concatenate([jnp.zeros((t, half), F32), sin, jnp.zeros((t, rest), F32)], axis=1)
    s_next = jnp.concatenate([-sin, jnp.zeros((t, half), F32), jnp.zeros((t, rest), F32)], axis=1)
    tile = lambda z: jnp.tile(z, (1, N_HEADS))
    return tile(c_head), tile(s_prev), tile(s_next)


def _qkprep_kernel(q_ref, k_ref, v_ref, cos_ref, sp_ref, sn_ref, qw_ref, kw_ref, ones_ref, qo_ref, ko_ref,
                   *cache_refs):
    half = ROT_DIM // 2
    cos, sp, sn = cos_ref[...], sp_ref[...], sn_ref[...]
    for src, w_ref, dst, scale in ((q_ref, qw_ref, qo_ref, HEAD_DIM ** -0.5), (k_ref, kw_ref, ko_ref, None)):
        x = src[...]
        ms = _dot_split(x * x, ones_ref[...], 2) * (1.0 / HEAD_DIM)
        xn = x * lax.rsqrt(ms + EPS) * w_ref[...]
        y = xn * cos + pltpu.roll(xn, half, 1) * sp + pltpu.roll(xn, D_ATT - half, 1) * sn
        dst[...] = y if scale is None else y * scale
    if cache_refs:
        kt_ref, vt_ref = cache_refs
        kt_ref[...] = ko_ref[...].T
        vt_ref[...] = v_ref[...].T


def _qkprep(qkv, row0, nrows, tables, table_blocks, qw, kw, tm, cache_batch=0):
    cos, sp, sn = tables
    rb0 = row0 // tm
    tab_spec = pl.BlockSpec((tm, D_ATT), lambda i: (i % table_blocks, 0))
    vec = pl.BlockSpec((1, D_ATT), lambda i: (0, 0))
    out_specs = [pl.BlockSpec((tm, D_ATT), lambda i: (i, 0))] * 2
    out_shape = [jax.ShapeDtypeStruct((nrows, D_ATT), F32)] * 2
    if cache_batch:
        out_specs += [pl.BlockSpec((None, D_ATT, tm), lambda i: (i // table_blocks, 0, i % table_blocks))] * 2
        out_shape += [jax.ShapeDtypeStruct((cache_batch, D_ATT, table_blocks * tm), F32)] * 2
    return pl.pallas_call(
        _qkprep_kernel,
        grid=(nrows // tm,),
        in_specs=[pl.BlockSpec((tm, D_ATT), lambda i: (rb0 + i, 0)),
                  pl.BlockSpec((tm, D_ATT), lambda i: (rb0 + i, 1)),
                  pl.BlockSpec((tm, D_ATT), lambda i: (rb0 + i, 2)),
                  tab_spec, tab_spec, tab_spec, vec, vec,
                  pl.BlockSpec((D_ATT, D_ATT), lambda i: (0, 0))],
        out_specs=out_specs,
        out_shape=out_shape,
        compiler_params=_cparams(("parallel",)),
        name="qkprep",
    )(qkv, qkv, qkv, cos, sp, sn, jnp.tile(qw, N_HEADS).reshape(1, D_ATT),
      jnp.tile(kw, N_HEADS).reshape(1, D_ATT), _head_ones())


def _multiplicity(q_pos, k_pos):
    d = q_pos[:, None] - k_pos[None, :]
    c = np.zeros(d.shape, np.float32)
    for window, dil in PATTERNS:
        c += ((d >= 0) & (d <= window) & (d % dil == 0) & (k_pos[None, :] >= 0)).astype(np.float32)
    return c


def _attn_prompt_kernel(q_ref, k_ref, v_ref, mask_ref, o_ref, *, tq):
    qi = pl.program_id(1)
    lane = lax.broadcasted_iota(jnp.int32, (1, LANES), 1)
    first = lane < HEAD_DIM
    qms = []
    for hp in range(HEAD_PAIRS):
        qp = q_ref[:, hp * LANES:(hp + 1) * LANES]
        qms.append(jnp.where(first, qp, 0.0).astype(BF16))
        qms.append(jnp.where(first, 0.0, qp).astype(BF16))

    def body(kj, carry):
        rows = pl.ds(pl.multiple_of(kj * tq, tq), tq)
        c = mask_ref[qi - kj]
        valid = c > 0.0
        new = []
        pair = lambda h: slice((h // 2) * LANES, (h // 2 + 1) * LANES)
        scores = [_dot_nt(qms[h], k_ref[rows, pair(h)]) for h in range(N_HEADS)]
        for h in range(N_HEADS):
            sl = pair(h)
            m, l, acc = carry[h]
            s = scores[h]
            sm = jnp.where(valid, s, NEG_BIG)
            m_new = jnp.maximum(m, jnp.max(sm, axis=1, keepdims=True))
            alpha = jnp.exp(m - m_new)
            p = jnp.exp(sm - m_new) * c
            l = alpha * l + jnp.sum(p, axis=1, keepdims=True)
            acc = alpha * acc + _dot(p, v_ref[rows, sl])
            new.append((m_new, l, acc))
        return tuple(new)

    init = tuple((jnp.full((tq, 1), NEG_BIG, F32), jnp.zeros((tq, 1), F32), jnp.zeros((tq, LANES), F32))
                 for _ in range(N_HEADS))
    res = lax.fori_loop(0, qi + 1, body, init)
    for hp in range(HEAD_PAIRS):
        (_, l0, a0), (_, l1, a1) = res[2 * hp], res[2 * hp + 1]
        o_ref[:, hp * LANES:(hp + 1) * LANES] = jnp.where(first, a0 / l0, a1 / l1)


def _attn_prompt(q, k, qkv, batch, seq, tq=256):
    nb = seq // tq
    pos = np.arange(tq)
    masks = np.stack([_multiplicity(pos + d * tq, pos) for d in range(nb)])
    return pl.pallas_call(
        functools.partial(_attn_prompt_kernel, tq=tq),
        grid=(batch, nb),
        in_specs=[pl.BlockSpec((tq, D_ATT), lambda b, i: (b * nb + i, 0)),
                  pl.BlockSpec((seq, D_ATT), lambda b, i: (b, 0)),
                  pl.BlockSpec((seq, D_ATT), lambda b, i: (b, 2)),
                  pl.BlockSpec((nb, tq, tq), lambda b, i: (0, 0, 0))],
        out_specs=pl.BlockSpec((tq, D_ATT), lambda b, i: (b * nb + i, 0)),
        out_shape=jax.ShapeDtypeStruct((batch * seq, D_ATT), F32),
        compiler_params=_cparams(("parallel", "arbitrary")),
        name="attn_prompt",
    )(q, k, qkv, jnp.asarray(masks))


def _attn_sample_kernel(q_ref, kn_ref, vn_ref, kt_ref, vt_ref, mp_ref, mn_ref, hm_ref, o_ref, *, tq):
    q = q_ref[...]
    hm = hm_ref[...]
    qbd = jnp.concatenate([q] * N_HEADS, axis=0) * hm
    s_past = _dot(qbd, kt_ref[...])
    s_new = _dot_nt(qbd, kn_ref[...])
    c_past, c_new = mp_ref[...], mn_ref[...]
    sm_past = jnp.where(c_past > 0.0, s_past, NEG_BIG)
    sm_new = jnp.where(c_new > 0.0, s_new, NEG_BIG)
    m = jnp.maximum(jnp.max(sm_past, axis=1, keepdims=True), jnp.max(sm_new, axis=1, keepdims=True))
    p_past = jnp.exp(sm_past - m) * c_past
    p_new = jnp.exp(sm_new - m) * c_new
    l = jnp.sum(p_past, axis=1, keepdims=True) + jnp.sum(p_new, axis=1, keepdims=True)
    o = (_dot_nt(p_past, vt_ref[...]) + _dot(p_new, vn_ref[...])) / l * hm
    out = o[0:tq]
    for h in range(1, N_HEADS):
        out = out + o[h * tq:(h + 1) * tq]
    o_ref[...] = out


def _attn_sample(q, k, qkv, row0, kt_all, vt_all, layer, batch, tq, past):
    q_pos = past + np.arange(tq)
    rep = lambda c: np.concatenate([c] * N_HEADS, axis=0)
    mp = rep(_multiplicity(q_pos, np.arange(past)))
    mn = rep(_multiplicity(q_pos, q_pos))
    row_head = np.arange(N_HEADS * tq) // tq
    hm = (row_head[:, None] == (np.arange(D_ATT) // HEAD_DIM)[None, :]).astype(np.float32)
    rb0 = row0 // tq
    new_spec = pl.BlockSpec((tq, D_ATT), lambda b: (b, 0))
    cache_spec = pl.BlockSpec((None, None, D_ATT, past), lambda b: (layer, b, 0, 0))
    full_spec = lambda a: pl.BlockSpec(a.shape, lambda b: (0,) * a.ndim)
    mp, mn, hm = jnp.asarray(mp), jnp.asarray(mn), jnp.asarray(hm)
    return pl.pallas_call(
        functools.partial(_attn_sample_kernel, tq=tq),
        grid=(batch,),
        in_specs=[new_spec, new_spec, pl.BlockSpec((tq, D_ATT), lambda b: (rb0 + b, 2)),
                  cache_spec, cache_spec, full_spec(mp), full_spec(mn), full_spec(hm)],
        out_specs=new_spec,
        out_shape=jax.ShapeDtypeStruct((batch * tq, D_ATT), F32),
        compiler_params=_cparams(("parallel",)),
        name="attn_sample",
    )(q, k, qkv, kt_all, vt_all, mp, mn, hm)


def _rglru_kernel(x_ref, g_ref, tail0_ref, h0_ref, cw_ref, cb_ref, wa_ref, ba_ref, wx_ref, bx_ref, lam_ref,
                  y_ref, tail_ref, hl_ref, xpad, a_s, b_s, hc, *, batch, tt):
    rows = tt * batch
    pad = (CONV_W - 1) * batch
    i = pl.program_id(0)

    @pl.when(i == 0)
    def _():
        xpad[0:pad, :] = tail0_ref[...]
        hc[...] = h0_ref[...]

    @pl.when(i > 0)
    def _():
        xpad[0:pad, :] = xpad[rows:rows + pad, :]

    xpad[pad:pad + rows, :] = x_ref[...]
    xc = cb_ref[...] + cw_ref[0:1, :] * xpad[0:rows, :]
    for j in range(1, CONV_W):
        xc = xc + cw_ref[j:j + 1, :] * xpad[j * batch:j * batch + rows, :]
    r = jax.nn.sigmoid(_dot(xc, wa_ref[...]) + ba_ref[...])
    ig = jax.nn.sigmoid(_dot(xc, wx_ref[...]) + bx_ref[...])
    log_a = (-LRU_C) * r * _softplus(-lam_ref[...])
    a = jnp.exp(log_a)
    a_s[...] = a
    b_s[...] = jnp.sqrt(-jnp.tanh(log_a) * (a * a + 1.0)) * (ig * xc)

    def step(t, h):
        sl = pl.ds(pl.multiple_of(t * batch, batch), batch)
        h = a_s[sl, :] * h + b_s[sl, :]
        a_s[sl, :] = h
        return h

    h = lax.fori_loop(0, tt, step, hc[...])
    hc[...] = h
    y_ref[...] = a_s[...] * jax.nn.gelu(g_ref[...])
    tail_ref[...] = xpad[rows:rows + pad, :]
    hl_ref[...] = h


def _block_diag(w):
    nb, n, _ = w.shape
    eye = jnp.eye(nb, dtype=w.dtype)
    return (w[:, :, None, :] * eye[:, None, :, None]).reshape(nb * n, nb * n)


def _rglru(x_tm, g_tm, tail0, h0, p, batch, seq, tt):
    rows = tt * batch
    pad = (CONV_W - 1) * batch
    row_spec = pl.BlockSpec((rows, D_ATT), lambda i: (i, 0))
    const = lambda shape: pl.BlockSpec(shape, lambda i: (0,) * len(shape))
    vec = const((1, D_ATT))
    return pl.pallas_call(
        functools.partial(_rglru_kernel, batch=batch, tt=tt),
        grid=(seq // tt,),
        in_specs=[row_spec, row_spec, const((pad, D_ATT)), const((batch, D_ATT)), const((CONV_W, D_ATT)), vec,
                  const((D_ATT, D_ATT)), vec, const((D_ATT, D_ATT)), vec, vec],
        out_specs=[row_spec, const((pad, D_ATT)), const((batch, D_ATT))],
        out_shape=[jax.ShapeDtypeStruct((seq * batch, D_ATT), F32),
                   jax.ShapeDtypeStruct((pad, D_ATT), F32),
                   jax.ShapeDtypeStruct((batch, D_ATT), F32)],
        scratch_shapes=[pltpu.VMEM((pad + rows, D_ATT), F32), pltpu.VMEM((rows, D_ATT), F32),
                        pltpu.VMEM((rows, D_ATT), F32), pltpu.VMEM((batch, D_ATT), F32)],
        compiler_params=_cparams(("arbitrary",)),
        name="rglru",
    )(x_tm, g_tm, tail0, h0, p["conv_w"], p["conv_b"].reshape(1, D_ATT),
      _block_diag(p["rg_wa"]).astype(BF16), p["rg_ba"].reshape(1, D_ATT),
      _block_diag(p["rg_wx"]).astype(BF16), p["rg_bx"].reshape(1, D_ATT), p["rg_lambda"].reshape(1, D_ATT))


def _rwkv_pre_kernel(c_ref, prev_ref, mu_ref, w0_ref, w2_ref, a0_ref, a2_ref, g2_ref, kk_ref, ka_ref, ones_ref,
                     r_ref, k_ref, v_ref, kap_ref, a_ref, lw_ref, g_ref):
    c = c_ref[...]
    cm = c + (prev_ref[...] - c) * mu_ref[...]
    r = cm[:, 0:D_ATT]
    k = cm[:, D_ATT:2 * D_ATT]
    v = cm[:, 2 * D_ATT:3 * D_ATT]
    lo = cm[:, 3 * D_ATT:3 * D_ATT + LORA_W + LORA_A]
    g_lo = cm[:, 3 * D_ATT + LORA_W + LORA_A:]
    w = -_softplus(-(w0_ref[...] + _dot(jnp.tanh(lo), w2_ref[...]))) - 0.5
    a = jax.nn.sigmoid(a0_ref[...] + _dot(lo, a2_ref[...]))
    g = _dot(jax.nn.sigmoid(g_lo), g2_ref[...])
    kk = k * kk_ref[...]
    nrm = jnp.sqrt(_dot_split(kk * kk, ones_ref[...], 2))
    r_ref[...] = r
    k_ref[...] = k * (1.0 + (a - 1.0) * ka_ref[...])
    v_ref[...] = v
    kap_ref[...] = kk / jnp.maximum(nrm, 1e-12)
    a_ref[...] = a
    lw_ref[...] = -jnp.exp(w)
    g_ref[...] = g


def _rwkv_pre(cc, prev, p, tm=256):
    n = cc.shape[0]
    lo = LORA_W + LORA_A
    w2p = jnp.concatenate([p["rwkv_w2"], jnp.zeros((LORA_A, D_ATT), F32)], axis=0).astype(BF16)
    a2p = jnp.concatenate([jnp.zeros((LORA_W, D_ATT), F32), p["rwkv_a2"]], axis=0).astype(BF16)
    row = pl.BlockSpec((tm, C_COLS), lambda i: (i, 0))
    const = lambda shape: pl.BlockSpec(shape, lambda i: (0,) * len(shape))
    vec = const((1, D_ATT))
    out = pl.BlockSpec((tm, D_ATT), lambda i: (i, 0))
    return pl.pallas_call(
        _rwkv_pre_kernel,
        grid=(n // tm,),
        in_specs=[row, row, const((1, C_COLS)), vec, const((lo, D_ATT)), vec, const((lo, D_ATT)),
                  const((LORA_G, D_ATT)), vec, vec, const((D_ATT, D_ATT))],
        out_specs=[out] * 7,
        out_shape=[jax.ShapeDtypeStruct((n, D_ATT), F32)] * 7,
        compiler_params=_cparams(("parallel",)),
        name="rwkv_pre",
    )(cc, prev, p["rwkv_mu"].reshape(1, C_COLS), p["rwkv_w0"].reshape(1, D_ATT), w2p,
      p["rwkv_a0"].reshape(1, D_ATT), a2p, p["rwkv_g2"].astype(BF16), p["rwkv_kk"].reshape(1, D_ATT),
      p["rwkv_ka"].reshape(1, D_ATT), _head_ones())


def _wkv_kernel(*refs, c, bb):
    chunks = [refs[j * bb:(j + 1) * bb] for j in range(7)]
    (s0_ref, rk_ref, lnw_ref, lnb_ref, tri_ref, sl_ref, il_ref, eye_ref, bm_ref, ones_ref,
     o_ref, sl_out_ref, s_scr, y_scr) = refs[7 * bb:]
    ci = pl.program_id(1)

    @pl.when(ci == 0)
    def _():
        s_scr[...] = s0_ref[...]

    rk, lnw, lnb, tri, strict, incl, eye, bmask, ones = (
        rk_ref[...], lnw_ref[...], lnb_ref[...], tri_ref[...], sl_ref[...], il_ref[...], eye_ref[...],
        bm_ref[...], ones_ref[...])
    lane = lax.broadcasted_iota(jnp.int32, (1, LANES), 1)
    first = lane < HEAD_DIM
    stack = lambda z: jnp.concatenate([jnp.where(first, z, 0.0), jnp.where(first, 0.0, z)], axis=0)
    twice = lambda z: jnp.concatenate([z, z], axis=0)
    fold = lambda z: z[:c] + z[c:]
    n_fac = int(np.log2(c))
    two_c = 2 * c

    seqs = []
    for bi in range(bb):
        r, k, v, kap, a, lw, g = (chunks[j][bi][0] for j in range(7))
        cum = _split_dot_left(tri, lw, 3)
        tot = cum[c - 1:c, :]
        e_neg, e_end = jnp.exp(-cum), jnp.exp(tot - cum)
        b = kap * a
        seqs.append(dict(r=r, k=k, v=v, g=g, kap_t=kap * jnp.exp(cum - lw), r_t=r * jnp.exp(cum), k_t=k * e_neg,
                         b_t=b * e_neg, k_h=k * e_end, b_h=b * e_end, g_end=jnp.exp(tot)))
    chains = [(bi, hp, slice(hp * LANES, (hp + 1) * LANES)) for bi in range(bb) for hp in range(HEAD_PAIRS)]

    grams, states, v_stacks = [], [], []
    for bi, hp, sl in chains:
        q = seqs[bi]
        grams.append(_dot_nt(jnp.concatenate([stack(q["kap_t"][:, sl]), stack(q["r_t"][:, sl])], axis=0),
                             jnp.concatenate([twice(q["k_t"][:, sl]), twice(q["b_t"][:, sl])], axis=0)))
        states.append(s_scr[bi, hp])
        v_stacks.append(stack(q["v"][:, sl]))
    from_state = [_dot_nt(jnp.concatenate([seqs[bi]["kap_t"][:, sl], seqs[bi]["r_t"][:, sl]], axis=0), s_p)
                  for (bi, hp, sl), s_p in zip(chains, states)]
    a_kk = [gm[:two_c, :two_c] * strict for gm in grams]
    y_mat = [jnp.concatenate([gm[two_c:, :two_c] * incl, -(gm[two_c:, two_c:] * incl)], axis=1) for gm in grams]
    pw = [-(gm[:two_c, two_c:] * strict) for gm in grams]
    inv = [eye + m for m in pw]
    for _ in range(n_fac - 1):
        pw = [_dot(m, m) for m in pw]
        inv = [t + _dot(t, m) for t, m in zip(inv, pw)]
    x_s = [stack(fs[:c]) + _dot(am, vs) for fs, am, vs in zip(from_state, a_kk, v_stacks)]
    u_s = [_dot(t, x) for t, x in zip(inv, x_s)]
    y_s = [stack(fs[c:]) + _dot(ym, jnp.concatenate([vs, u], axis=0))
           for fs, ym, vs, u in zip(from_state, y_mat, v_stacks, u_s)]
    upd = [_dot(jnp.concatenate([seqs[bi]["v"][:, sl], fold(u)], axis=0).T,
                jnp.concatenate([seqs[bi]["k_h"][:, sl], -seqs[bi]["b_h"][:, sl]], axis=0))
           for (bi, hp, sl), u in zip(chains, u_s)]
    for (bi, hp, sl), s_p, d_s, y in zip(chains, states, upd, y_s):
        s_scr[bi, hp] = s_p * seqs[bi]["g_end"][:, sl] + bmask * d_s
        y_scr[bi, :, sl] = fold(y)

    inv_n = 1.0 / HEAD_DIM
    for bi in range(bb):
        q = seqs[bi]
        y = y_scr[bi]
        mean = _dot_split(y, ones, 3) * inv_n
        d = y - mean
        var = _dot_split(d * d, ones, 3) * inv_n
        yn = d * lax.rsqrt(var + GN_EPS) * lnw + lnb
        bonus = _dot_split(q["r"] * q["k"] * rk, ones, 3) * q["v"]
        o_ref[bi, 0] = (yn + bonus) * q["g"]

    @pl.when(ci == pl.num_programs(1) - 1)
    def _():
        sl_out_ref[...] = s_scr[...]


def _wkv(parts, row0, s0_pairs, p, batch, seq, c=64, bb=4):
    assert 2 * c == LANES
    nc = seq // c
    rb0 = row0 // c
    n3 = [z.reshape(z.shape[0] // c, c, D_ATT) for z in parts]
    idx = np.arange(2 * c)
    same = (idx[:, None] // c) == (idx[None, :] // c)
    strict = (same & (idx[None, :] % c < idx[:, None] % c)).astype(np.float32)
    incl = (same & (idx[None, :] % c <= idx[:, None] % c)).astype(np.float32)
    eye = np.eye(2 * c, dtype=np.float32)
    half = np.arange(LANES) // HEAD_DIM
    bmask = (half[:, None] == half[None, :]).astype(np.float32)
    tri = (np.arange(c)[None, :] <= np.arange(c)[:, None]).astype(np.float32)
    def chunk(j):
        return pl.BlockSpec((1, c, D_ATT), lambda g, i: (rb0 + (g * bb + j) * nc + i, 0, 0))

    const = lambda shape: pl.BlockSpec(shape, lambda g, i: (0,) * len(shape))
    vec = const((1, D_ATT))
    state = pl.BlockSpec((bb, HEAD_PAIRS, LANES, LANES), lambda g, i: (g, 0, 0, 0))
    operands = [z for z in n3 for _ in range(bb)]
    return pl.pallas_call(
        functools.partial(_wkv_kernel, c=c, bb=bb),
        grid=(batch // bb, nc),
        in_specs=[chunk(j) for _ in range(7) for j in range(bb)]
        + [state, vec, vec, vec, const((c, c)), const((2 * c, 2 * c)), const((2 * c, 2 * c)),
           const((2 * c, 2 * c)), const((LANES, LANES)), const((D_ATT, D_ATT))],
        out_specs=[pl.BlockSpec((bb, 1, c, D_ATT), lambda g, i: (g, i, 0, 0)), state],
        out_shape=[jax.ShapeDtypeStruct((batch, nc, c, D_ATT), F32),
                   jax.ShapeDtypeStruct((batch, HEAD_PAIRS, LANES, LANES), F32)],
        scratch_shapes=[pltpu.VMEM((bb, HEAD_PAIRS, LANES, LANES), F32), pltpu.VMEM((bb, c, D_ATT), F32)],
        compiler_params=_cparams(("parallel", "arbitrary")),
        name="wkv",
    )(*operands, s0_pairs, p["rwkv_rk"].reshape(1, D_ATT), p["rwkv_lnw"].reshape(1, D_ATT),
      p["rwkv_lnb"].reshape(1, D_ATT), jnp.asarray(tri, dtype=BF16), jnp.asarray(strict), jnp.asarray(incl),
      jnp.asarray(eye), jnp.asarray(bmask), _head_ones())


def _wkv_lane_kernel(r_ref, k_ref, v_ref, kap_ref, a_ref, lw_ref, g_ref, s0_ref, rk_ref, lnw_ref, lnb_ref,
                     o_ref, s_ref, *, steps):
    s_ref[...] = s0_ref[...]
    inv_n = 1.0 / HEAD_DIM
    for t in range(steps):
        kap = kap_ref[t]
        w = jnp.exp(lw_ref[t])
        bt = kap * a_ref[t]
        kt = k_ref[t]
        rt = r_ref[t]

        def per_v(vi, _, t=t, kap=kap, w=w, bt=bt, kt=kt, rt=rt):
            s = s_ref[vi]
            sa = -jnp.sum(s * kap, axis=0, keepdims=True)
            vv = v_ref[t, pl.ds(vi, 1), :]
            s = s * w + sa * bt + vv * kt
            s_ref[vi] = s
            o_ref[t, pl.ds(vi, 1), :] = jnp.sum(s * rt, axis=0, keepdims=True)
            return 0

        lax.fori_loop(0, HEAD_DIM, per_v, 0, unroll=4)
        y = o_ref[t]
        mean = jnp.sum(y, axis=0, keepdims=True) * inv_n
        d = y - mean
        var = jnp.sum(d * d, axis=0, keepdims=True) * inv_n
        yn = d * lax.rsqrt(var + GN_EPS) * lnw_ref[...] + lnb_ref[...]
        bonus = jnp.sum(rt * kt * rk_ref[...], axis=0, keepdims=True) * v_ref[t]
        o_ref[t] = (yn + bonus) * g_ref[t]


def _wkv_lane(parts_t, s0_t, p, batch, steps):
    tok = pl.BlockSpec((steps, HEAD_DIM, batch), lambda h: (0, h, 0))
    st = pl.BlockSpec((HEAD_DIM, HEAD_DIM, batch), lambda h: (h, 0, 0))
    col = pl.BlockSpec((HEAD_DIM, batch), lambda h: (h, 0))
    bcast = lambda z: jnp.broadcast_to(z.reshape(D_ATT, 1), (D_ATT, batch))
    return pl.pallas_call(
        functools.partial(_wkv_lane_kernel, steps=steps),
        grid=(N_HEADS,),
        in_specs=[tok] * 7 + [st, col, col, col],
        out_specs=[tok, st],
        out_shape=[jax.ShapeDtypeStruct((steps, D_ATT, batch), F32),
                   jax.ShapeDtypeStruct((D_ATT, HEAD_DIM, batch), F32)],
        compiler_params=_cparams(("parallel",)),
        name="wkv_lane",
    )(*parts_t, s0_t, bcast(p["rwkv_rk"]), bcast(p["rwkv_lnw"]), bcast(p["rwkv_lnb"]))


def _pair_states(s):
    b = s.shape[0]
    s = s.reshape(b, HEAD_PAIRS, 2, HEAD_DIM, HEAD_DIM)
    z = jnp.zeros((b, HEAD_PAIRS, HEAD_DIM, HEAD_DIM), s.dtype)
    top = jnp.concatenate([s[:, :, 0], z], axis=-1)
    bot = jnp.concatenate([z, s[:, :, 1]], axis=-1)
    return jnp.concatenate([top, bot], axis=-2)


def _unpair_states(sp):
    b = sp.shape[0]
    first = sp[:, :, :HEAD_DIM, :HEAD_DIM]
    second = sp[:, :, HEAD_DIM:, HEAD_DIM:]
    return jnp.stack([first, second], axis=2).reshape(b, N_HEADS, HEAD_DIM, HEAD_DIM)


def _merge_kernel(oa_ref, ob_ref, oc_ref, gl_ref, x_ref, bm_ref, wa_ref, wb_ref, wc_ref, wo_ref, nw_ref,
                  wgh_ref, wgl_ref, bg_ref, x1_ref, xn_ref, comb_ref):
    merged = None
    for j, (o_ref, w_ref) in enumerate(((oa_ref, wa_ref), (ob_ref, wb_ref), (oc_ref, wc_ref))):
        cols = slice(j * D_MODEL, (j + 1) * D_MODEL)
        gate = jax.nn.sigmoid(gl_ref[:, cols] + bm_ref[:, cols])
        t = gate * _dot(o_ref[...], w_ref[...])
        merged = t if merged is None else merged + t
    x1 = x_ref[...] + _dot(merged, wo_ref[...])
    x1_ref[...] = x1
    xn = x1 * lax.rsqrt(jnp.mean(x1 * x1, axis=-1, keepdims=True) + EPS) * nw_ref[...]
    xn_ref[...] = xn.astype(BF16)
    hi = xn.astype(BF16)
    lo = (xn - hi.astype(F32)).astype(BF16)
    wgh, wgl = wgh_ref[...], wgl_ref[...]
    lg = (jnp.dot(hi, wgh, preferred_element_type=F32) + jnp.dot(lo, wgh, preferred_element_type=F32)
          + jnp.dot(hi, wgl, preferred_element_type=F32)) + bg_ref[...]
    col = lambda j: lg[:, j:j + 1]
    best = col(0)
    gi = jnp.zeros(best.shape, jnp.int32)
    for j in range(1, N_GROUPS):
        better = col(j) > best
        best = jnp.where(better, col(j), best)
        gi = jnp.where(better, j, gi)
    den = jnp.zeros(best.shape, F32)
    for j in range(N_GROUPS):
        den = den + jnp.exp(col(j) - best)
    p_group = 1.0 / den
    le = []
    for j in range(EXP_PER_GROUP):
        z = col(N_GROUPS + j)
        for grp in range(1, N_GROUPS):
            z = jnp.where(gi == grp, col(N_GROUPS + grp * EXP_PER_GROUP + j), z)
        le.append(z)
    v1 = le[0]
    i1 = jnp.zeros(best.shape, jnp.int32)
    for j in range(1, EXP_PER_GROUP):
        better = le[j] > v1
        v1 = jnp.where(better, le[j], v1)
        i1 = jnp.where(better, j, i1)
    v2 = jnp.full(best.shape, -jnp.inf, F32)
    i2 = jnp.zeros(best.shape, jnp.int32)
    for j in range(EXP_PER_GROUP):
        cand = jnp.where(i1 == j, -jnp.inf, le[j])
        better = cand > v2
        v2 = jnp.where(better, cand, v2)
        i2 = jnp.where(better, j, i2)
    e2 = jnp.exp(v2 - v1)
    w1 = p_group / (1.0 + e2)
    w2 = p_group * e2 / (1.0 + e2)
    lane = lax.broadcasted_iota(jnp.int32, lg.shape, 1)
    base = gi * EXP_PER_GROUP
    comb_ref[...] = jnp.where(lane == base + i1, w1, 0.0) + jnp.where(lane == base + i2, w2, 0.0)


def _merge(oa, ob, oc, gl, x, p, tm=256):
    n = x.shape[0]
    wg = jnp.concatenate([p["moe_wg_group"], p["moe_wg_exp"]], axis=1)
    wg = jnp.pad(wg, ((0, 0), (0, LANES - wg.shape[1])))
    wgh = wg.astype(BF16)
    wgl = (wg - wgh.astype(F32)).astype(BF16)
    bg = jnp.pad(jnp.concatenate([p["moe_bg_group"], p["moe_bg_exp"]]), (0, LANES - N_GROUPS - N_EXPERTS))
    row = lambda w: pl.BlockSpec((tm, w), lambda i: (i, 0))
    const = lambda shape: pl.BlockSpec(shape, lambda i: (0,) * len(shape))
    return pl.pallas_call(
        _merge_kernel,
        grid=(n // tm,),
        in_specs=[row(D_ATT), row(D_ATT), row(D_ATT), row(GL_COLS), row(D_MODEL), const((1, GL_COLS)),
                  const((D_ATT, D_MODEL)), const((D_ATT, D_MODEL)), const((D_ATT, D_MODEL)),
                  const((D_MODEL, D_MODEL)), const((1, D_MODEL)), const((D_MODEL, LANES)),
                  const((D_MODEL, LANES)), const((1, LANES))],
        out_specs=[row(D_MODEL), row(D_MODEL), row(LANES)],
        out_shape=[jax.ShapeDtypeStruct((n, D_MODEL), F32), jax.ShapeDtypeStruct((n, D_MODEL), BF16),
                   jax.ShapeDtypeStruct((n, LANES), F32)],
        compiler_params=_cparams(("parallel",)),
        name="merge",
    )(oa, ob, oc, gl, x, p["b_merge"].reshape(1, GL_COLS), p["w_branch_a"].astype(BF16),
      p["w_branch_b"].astype(BF16), p["w_branch_c"].astype(BF16), p["w_out"].astype(BF16),
      p["norm2_w"].reshape(1, D_MODEL), wgh, wgl, bg.reshape(1, LANES))


def _moe_kernel(xn_ref, comb_ref, x1_ref, w1_ref, w3_ref, w2_ref, o_ref, acc):
    e = pl.program_id(1)

    @pl.when(e == 0)
    def _():
        acc[...] = x1_ref[...]

    xb = xn_ref[...]
    lane = lax.broadcasted_iota(jnp.int32, comb_ref.shape, 1)
    wt = jnp.sum(jnp.where(lane == e, comb_ref[...], 0.0), axis=1, keepdims=True)
    h = jax.nn.silu(jnp.dot(xb, w1_ref[0], preferred_element_type=F32)) * jnp.dot(xb, w3_ref[0],
                                                                                  preferred_element_type=F32)
    acc[...] += _dot(h * wt, w2_ref[0])

    @pl.when(e == pl.num_programs(1) - 1)
    def _():
        o_ref[...] = acc[...]


def _moe(xn, comb, x1, w1, w3, w2, tm=1024):
    n = xn.shape[0]
    return pl.pallas_call(
        _moe_kernel,
        grid=(n // tm, N_EXPERTS),
        in_specs=[pl.BlockSpec((tm, D_MODEL), lambda i, e: (i, 0)),
                  pl.BlockSpec((tm, LANES), lambda i, e: (i, 0)),
                  pl.BlockSpec((tm, D_MODEL), lambda i, e: (i, 0)),
                  pl.BlockSpec((1, D_MODEL, D_EXPERT), lambda i, e: (e, 0, 0)),
                  pl.BlockSpec((1, D_MODEL, D_EXPERT), lambda i, e: (e, 0, 0)),
                  pl.BlockSpec((1, D_EXPERT, D_MODEL), lambda i, e: (e, 0, 0))],
        out_specs=pl.BlockSpec((tm, D_MODEL), lambda i, e: (i, 0)),
        out_shape=jax.ShapeDtypeStruct((n, D_MODEL), F32),
        scratch_shapes=[pltpu.VMEM((tm, D_MODEL), F32)],
        compiler_params=_cparams(("parallel", "arbitrary")),
        name="moe",
    )(xn, comb, x1, w1, w3, w2)


def _time_major(z, batch, seq):
    return z.reshape(batch, seq, z.shape[-1]).transpose(1, 0, 2).reshape(seq * batch, z.shape[-1])


def _batch_major(z, batch, seq):
    return z.reshape(seq, batch, z.shape[-1]).transpose(1, 0, 2).reshape(batch * seq, z.shape[-1])


def _layer(x, groups, p, caches, layer, tabs):
    qkv, rg, cc, gl = _inproj(x, p["norm1_w"], p["w_in"].astype(BF16))
    n = x.shape[0]

    prev_parts = []
    for g in groups:
        c3 = cc[g["row0"]:g["row0"] + g["batch"] * g["seq"]].reshape(g["batch"], g["seq"], C_COLS)
        prev_parts.append(jnp.concatenate([g["shift"][:, None, :], c3[:, :-1]], axis=1).reshape(-1, C_COLS))
    prev = jnp.concatenate(prev_parts, axis=0)
    wkv_parts = _rwkv_pre(cc, prev, p)

    oa, ob, oc, new_states = [], [], [], []
    for gi, g in enumerate(groups):
        row0, batch, seq = g["row0"], g["batch"], g["seq"]
        rows = batch * seq
        tm = 256
        heads = lambda z: z.reshape(batch, seq, N_HEADS, HEAD_DIM)
        if g["past"] == 0:
            q, k, kt, vt = _qkprep(qkv, row0, rows, tabs[gi], g["table_blocks"], p["q_norm_w"], p["k_norm_w"], tm,
                                   cache_batch=batch)
            o_att = _attn_prompt(q, k, qkv, batch, seq)
            from_t = lambda z: z.reshape(batch, N_HEADS, HEAD_DIM, seq).transpose(0, 3, 1, 2)
            k_new, v_new = from_t(kt), from_t(vt)
        else:
            q, k = _qkprep(qkv, row0, rows, tabs[gi], g["table_blocks"], p["q_norm_w"], p["k_norm_w"], tm)
            o_att = _attn_sample(q, k, qkv, row0, caches[0], caches[1], layer, batch, seq, g["past"])
            k_new, v_new = heads(k), heads(qkv[row0:row0 + rows, 2 * D_ATT:])
        oa.append(o_att)
        xr = _time_major(rg[row0:row0 + rows, :D_ATT], batch, seq)
        gr = _time_major(rg[row0:row0 + rows, D_ATT:], batch, seq)
        tail0 = g["conv"].transpose(1, 0, 2).reshape((CONV_W - 1) * batch, D_ATT)
        y_tm, tail, h_last = _rglru(xr, gr, tail0, g["h"], p, batch, seq, g["tt"])
        ob.append(_batch_major(y_tm, batch, seq))
        conv_tail = tail.reshape(CONV_W - 1, batch, D_ATT).transpose(1, 0, 2)
        if g["past"] == 0:
            o_wkv, s_pairs = _wkv(wkv_parts, row0, _pair_states(g["wkv"]), p, batch, seq)
            oc.append(o_wkv.reshape(rows, D_ATT))
            s_last = _unpair_states(s_pairs)
        else:
            to_lanes = lambda z: z[row0:row0 + rows].reshape(batch, seq, D_ATT).transpose(1, 2, 0)
            o_t, s_t = _wkv_lane([to_lanes(z) for z in wkv_parts], g["wkv"], p, batch, seq)
            oc.append(o_t.transpose(2, 0, 1).reshape(rows, D_ATT))
            s_last = s_t.reshape(N_HEADS, HEAD_DIM, HEAD_DIM, batch).transpose(3, 0, 1, 2)
        shift_last = cc[row0:row0 + rows].reshape(batch, seq, C_COLS)[:, -1]
        new_states.append((k_new, v_new, conv_tail, h_last, shift_last, s_last))

    cat = lambda zs: jnp.concatenate(zs, axis=0)
    x1, xn2, comb = _merge(cat(oa), cat(ob), cat(oc), gl, x, p)
    x2 = _moe(xn2, comb, x1, p["moe_w1"].astype(BF16), p["moe_w3"].astype(BF16), p["moe_w2"].astype(BF16))
    return x2, new_states


def kernel(x_prompt, x_sample, cache_k_win, cache_v_win, state_conv, state_h, state_shift, state_wkv, norm1_w, w_in, q_norm_w, k_norm_w, conv_w, conv_b, rg_wa, rg_ba, rg_wx, rg_bx, rg_lambda, rwkv_mu, rwkv_w0, rwkv_w2, rwkv_a0, rwkv_a2, rwkv_g2, rwkv_kk, rwkv_ka, rwkv_rk, rwkv_lnw, rwkv_lnb, b_merge, w_branch_a, w_branch_b, w_branch_c, w_out, norm2_w, moe_wg_group, moe_bg_group, moe_wg_exp, moe_bg_exp, moe_w1, moe_w3, moe_w2):
    weights = dict(norm1_w=norm1_w, w_in=w_in, q_norm_w=q_norm_w, k_norm_w=k_norm_w, conv_w=conv_w, conv_b=conv_b,
                   rg_wa=rg_wa, rg_ba=rg_ba, rg_wx=rg_wx, rg_bx=rg_bx, rg_lambda=rg_lambda, rwkv_mu=rwkv_mu,
                   rwkv_w0=rwkv_w0, rwkv_w2=rwkv_w2, rwkv_a0=rwkv_a0, rwkv_a2=rwkv_a2, rwkv_g2=rwkv_g2,
                   rwkv_kk=rwkv_kk, rwkv_ka=rwkv_ka, rwkv_rk=rwkv_rk, rwkv_lnw=rwkv_lnw, rwkv_lnb=rwkv_lnb,
                   b_merge=b_merge, w_branch_a=w_branch_a, w_branch_b=w_branch_b, w_branch_c=w_branch_c,
                   w_out=w_out, norm2_w=norm2_w, moe_wg_group=moe_wg_group, moe_bg_group=moe_bg_group,
                   moe_wg_exp=moe_wg_exp, moe_bg_exp=moe_bg_exp, moe_w1=moe_w1, moe_w3=moe_w3, moe_w2=moe_w2)
    depth = w_in.shape[0]
    bp, tp = x_prompt.shape[0], x_prompt.shape[1]
    bs, ts = x_sample.shape[0], x_sample.shape[1]
    past = cache_k_win.shape[2]
    np_rows = bp * tp
    x = jnp.concatenate([x_prompt.reshape(np_rows, D_MODEL), x_sample.reshape(bs * ts, D_MODEL)], axis=0)
    tm = 256
    tab_p = _rope_tables(jnp.arange(tp, dtype=jnp.int32))
    tab_s = tuple(jnp.tile(t, (tm // ts, 1)) for t in _rope_tables(past + jnp.arange(ts, dtype=jnp.int32)))
    kt_all = cache_k_win.transpose(0, 1, 3, 4, 2).reshape(depth, bs, D_ATT, past)
    vt_all = cache_v_win.transpose(0, 1, 3, 4, 2).reshape(depth, bs, D_ATT, past)
    wkv_t = state_wkv.transpose(0, 2, 3, 4, 1).reshape(depth, D_ATT, HEAD_DIM, bs)
    outs_p = [[] for _ in range(6)]
    outs_s = [[] for _ in range(6)]
    for l in range(depth):
        p = {k_: v_[l] for k_, v_ in weights.items()}
        groups = [
            dict(row0=0, batch=bp, seq=tp, past=0, table_blocks=tp // tm, tt=64,
                 conv=jnp.zeros((bp, CONV_W - 1, D_ATT), F32), h=jnp.zeros((bp, D_ATT), F32),
                 shift=jnp.zeros((bp, C_COLS), F32), wkv=jnp.zeros((bp, N_HEADS, HEAD_DIM, HEAD_DIM), F32)),
            dict(row0=np_rows, batch=bs, seq=ts, past=past, table_blocks=1, tt=ts,
                 conv=state_conv[l], h=state_h[l], shift=state_shift[l], wkv=wkv_t[l]),
        ]
        x, (st_p, st_s) = _layer(x, groups, p, (kt_all, vt_all), l, (tab_p, tab_s))
        for j in range(6):
            outs_p[j].append(st_p[j])
            outs_s[j].append(st_s[j])
    y_prompt = x[:np_rows].reshape(bp, tp, D_MODEL)
    y_sample = x[np_rows:].reshape(bs, ts, D_MODEL)
    stack = lambda zs: jnp.stack(zs)
    return (y_prompt, y_sample, *[stack(z) for z in outs_p], *[stack(z) for z in outs_s])
```

```python
import functools

import numpy as np
import jax
import jax.numpy as jnp
from jax import lax
from jax.experimental import pallas as pl
from jax.experimental.pallas import tpu as pltpu

F32 = jnp.float32
BF16 = jnp.bfloat16

D_MODEL = 1024
N_HEADS = 8
HEAD_DIM = 64
D_ATT = N_HEADS * HEAD_DIM
ROT_DIM = HEAD_DIM // 4
ROPE_THETA = 500000.0
PATTERNS = ((128, 1), (512, 4), (2048, 16))
CONV_W = 4
LRU_C = 8.0
LORA_W, LORA_A, LORA_G = 64, 64, 128
C_COLS = 3 * D_ATT + LORA_W + LORA_A + LORA_G
N_BRANCH = 3
GL_COLS = N_BRANCH * D_MODEL
N_GROUPS, EXP_PER_GROUP, N_EXPERTS = 4, 4, 16
D_EXPERT = 512
EPS = 1e-6
GN_EPS = 64e-5

LANES = 128
SUBLANES = 8
VMEM_LIMIT = 56 * 1024 * 1024
HEAD_PAIRS = D_ATT // LANES
NEG_BIG = -1e30


def _cparams(sem):
    return pltpu.CompilerParams(dimension_semantics=sem, vmem_limit_bytes=VMEM_LIMIT)


def _dot(a, b):
    return jnp.dot(a.astype(BF16), b.astype(BF16), preferred_element_type=F32)


def _dot_nt(a, b):
    return lax.dot_general(a.astype(BF16), b.astype(BF16), (((1,), (1,)), ((), ())),
                           preferred_element_type=F32)


def _dot_split(x, w_bf16, passes):
    acc = None
    r = x
    for i in range(passes):
        p = r.astype(BF16)
        t = jnp.dot(p, w_bf16, preferred_element_type=F32)
        acc = t if acc is None else acc + t
        if i + 1 < passes:
            r = r - p.astype(F32)
    return acc


def _split_dot_left(w_exact_bf16, x, passes):
    acc = None
    r = x
    for i in range(passes):
        p = r.astype(BF16)
        t = jnp.dot(w_exact_bf16, p, preferred_element_type=F32)
        acc = t if acc is None else acc + t
        if i + 1 < passes:
            r = r - p.astype(F32)
    return acc


def _softplus(x):
    return jnp.maximum(x, 0.0) + jnp.log1p(jnp.exp(-jnp.abs(x)))


def _head_ones():
    h = np.arange(D_ATT) // HEAD_DIM
    return jnp.asarray((h[:, None] == h[None, :]).astype(np.float32), dtype=BF16)


def _inproj_kernel(x_ref, nw_ref, w_ref, qkv_ref, rg_ref, cc_ref, gl_ref):
    x = x_ref[...]
    xn = x * lax.rsqrt(jnp.mean(x * x, axis=-1, keepdims=True) + EPS) * nw_ref[...]
    xb = xn.astype(BF16)
    off = 0
    for ref in (qkv_ref, rg_ref, cc_ref, gl_ref):
        n = ref.shape[-1]
        ref[...] = jnp.dot(xb, w_ref[:, off:off + n], preferred_element_type=F32)
        off += n


def _inproj(x, norm_w, w_in_bf16, tm=256, time_major_rnn=None):
    n = x.shape[0]
    widths = (3 * D_ATT, 2 * D_ATT, C_COLS, GL_COLS)
    in_cols = sum(widths)
    out_specs = [pl.BlockSpec((tm, w), lambda i: (i, 0)) for w in widths]
    out_shape = [jax.ShapeDtypeStruct((n, w), F32) for w in widths]
    if time_major_rnn is not None:
        batch, seq = time_major_rnn
        nt = seq // tm
        out_specs[1] = pl.BlockSpec((tm, widths[1]), lambda i: (i % nt, i // nt))
        out_shape[1] = jax.ShapeDtypeStruct((seq, batch * widths[1]), F32)
    outs = list(pl.pallas_call(
        _inproj_kernel,
        grid=(n // tm,),
        in_specs=[pl.BlockSpec((tm, D_MODEL), lambda i: (i, 0)),
                  pl.BlockSpec((1, D_MODEL), lambda i: (0, 0)),
                  pl.BlockSpec((D_MODEL, in_cols), lambda i: (0, 0))],
        out_specs=out_specs,
        out_shape=out_shape,
        compiler_params=_cparams(("parallel",)),
        name="inproj",
    )(x, norm_w.reshape(1, D_MODEL), w_in_bf16))
    if time_major_rnn is not None:
        outs[1] = outs[1].reshape(n, widths[1])
    return outs


def _rope_tables(pos):
    half = ROT_DIM // 2
    inv = ROPE_THETA ** (-jnp.arange(half, dtype=F32) / half)
    ang = pos.astype(F32)[:, None] * inv[None, :]
    cos, sin = jnp.cos(ang), jnp.sin(ang)
    t = pos.shape[0]
    rest = HEAD_DIM - ROT_DIM
    c_head = jnp.concatenate([cos, cos, jnp.ones((t, rest), F32)], axis=1)
    s_prev = jnp.concatenate([jnp.zeros((t, half), F32), sin, jnp.zeros((t, rest), F32)], axis=1)
    s_next = jnp.concatenate([-sin, jnp.zeros((t, half), F32), jnp.zeros((t, rest), F32)], axis=1)
    tile = lambda z: jnp.tile(z, (1, N_HEADS))
    return tile(c_head), tile(s_prev), tile(s_next)


def _qkprep_kernel(q_ref, k_ref, v_ref, cos_ref, sp_ref, sn_ref, qw_ref, kw_ref, ones_ref, qo_ref, ko_ref,
                   *cache_refs):
    half = ROT_DIM // 2
    cos, sp, sn = cos_ref[...], sp_ref[...], sn_ref[...]
    for src, w_ref, dst, scale in ((q_ref, qw_ref, qo_ref, HEAD_DIM ** -0.5), (k_ref, kw_ref, ko_ref, None)):
        x = src[...]
        ms = _dot_split(x * x, ones_ref[...], 2) * (1.0 / HEAD_DIM)
        xn = x * lax.rsqrt(ms + EPS) * w_ref[...]
        y = xn * cos + pltpu.roll(xn, half, 1) * sp + pltpu.roll(xn, D_ATT - half, 1) * sn
        dst[...] = y if scale is None else y * scale
    if cache_refs:
        kt_ref, vt_ref = cache_refs
        kt_ref[...] = ko_ref[...].T
        vt_ref[...] = v_ref[...].T


def _qkprep(qkv, row0, nrows, tables, table_blocks, qw, kw, tm, cache_batch=0):
    cos, sp, sn = tables
    rb0 = row0 // tm
    tab_spec = pl.BlockSpec((tm, D_ATT), lambda i: (i % table_blocks, 0))
    vec = pl.BlockSpec((1, D_ATT), lambda i: (0, 0))
    out_specs = [pl.BlockSpec((tm, D_ATT), lambda i: (i, 0))] * 2
    out_shape = [jax.ShapeDtypeStruct((nrows, D_ATT), F32)] * 2
    if cache_batch:
        out_specs += [pl.BlockSpec((None, D_ATT, tm), lambda i: (i // table_blocks, 0, i % table_blocks))] * 2
        out_shape += [jax.ShapeDtypeStruct((cache_batch, D_ATT, table_blocks * tm), F32)] * 2
    return pl.pallas_call(
        _qkprep_kernel,
        grid=(nrows // tm,),
        in_specs=[pl.BlockSpec((tm, D_ATT), lambda i: (rb0 + i, 0)),
                  pl.BlockSpec((tm, D_ATT), lambda i: (rb0 + i, 1)),
                  pl.BlockSpec((tm, D_ATT), lambda i: (rb0 + i, 2)),
                  tab_spec, tab_spec, tab_spec, vec, vec,
                  pl.BlockSpec((D_ATT, D_ATT), lambda i: (0, 0))],
        out_specs=out_specs,
        out_shape=out_shape,
        compiler_params=_cparams(("parallel",)),
        name="qkprep",
    )(qkv, qkv, qkv, cos, sp, sn, jnp.tile(qw, N_HEADS).reshape(1, D_ATT),
      jnp.tile(kw, N_HEADS).reshape(1, D_ATT), _head_ones())


def _multiplicity(q_pos, k_pos):
    d = q_pos[:, None] - k_pos[None, :]
    c = np.zeros(d.shape, np.float32)
    for window, dil in PATTERNS:
        c += ((d >= 0) & (d <= window) & (d % dil == 0) & (k_pos[None, :] >= 0)).astype(np.float32)
    return c


def _attn_prompt_kernel(q_ref, k_ref, v_ref, mask_ref, o_ref, *, tq):
    qi = pl.program_id(1)
    lane = lax.broadcasted_iota(jnp.int32, (1, LANES), 1)
    first = lane < HEAD_DIM
    qms = []
    for hp in range(HEAD_PAIRS):
        qp = q_ref[:, hp * LANES:(hp + 1) * LANES]
        qms.append(jnp.where(first, qp, 0.0).astype(BF16))
        qms.append(jnp.where(first, 0.0, qp).astype(BF16))

    def body(kj, carry):
        rows = pl.ds(pl.multiple_of(kj * tq, tq), tq)
        c = mask_ref[qi - kj]
        valid = c > 0.0
        new = []
        pair = lambda h: slice((h // 2) * LANES, (h // 2 + 1) * LANES)
        scores = [_dot_nt(qms[h], k_ref[rows, pair(h)]) for h in range(N_HEADS)]
        for h in range(N_HEADS):
            sl = pair(h)
            m, l, acc = carry[h]
            s = scores[h]
            sm = jnp.where(valid, s, NEG_BIG)
            m_new = jnp.maximum(m, jnp.max(sm, axis=1, keepdims=True))
            alpha = jnp.exp(m - m_new)
            p = jnp.exp(sm - m_new) * c
            l = alpha * l + jnp.sum(p, axis=1, keepdims=True)
            acc = alpha * acc + _dot(p, v_ref[rows, sl])
            new.append((m_new, l, acc))
        return tuple(new)

    init = tuple((jnp.full((tq, 1), NEG_BIG, F32), jnp.zeros((tq, 1), F32), jnp.zeros((tq, LANES), F32))
                 for _ in range(N_HEADS))
    res = lax.fori_loop(0, qi + 1, body, init)
    for hp in range(HEAD_PAIRS):
        (_, l0, a0), (_, l1, a1) = res[2 * hp], res[2 * hp + 1]
        o_ref[:, hp * LANES:(hp + 1) * LANES] = jnp.where(first, a0 / l0, a1 / l1)


def _attn_prompt(q, k, qkv, batch, seq, tq=256):
    nb = seq // tq
    pos = np.arange(tq)
    masks = np.stack([_multiplicity(pos + d * tq, pos) for d in range(nb)])
    return pl.pallas_call(
        functools.partial(_attn_prompt_kernel, tq=tq),
        grid=(batch, nb),
        in_specs=[pl.BlockSpec((tq, D_ATT), lambda b, i: (b * nb + i, 0)),
                  pl.BlockSpec((seq, D_ATT), lambda b, i: (b, 0)),
                  pl.BlockSpec((seq, D_ATT), lambda b, i: (b, 2)),
                  pl.BlockSpec((nb, tq, tq), lambda b, i: (0, 0, 0))],
        out_specs=pl.BlockSpec((tq, D_ATT), lambda b, i: (b * nb + i, 0)),
        out_shape=jax.ShapeDtypeStruct((batch * seq, D_ATT), F32),
        compiler_params=_cparams(("parallel", "arbitrary")),
        name="attn_prompt",
    )(q, k, qkv, jnp.asarray(masks))


def _attn_sample_kernel(q_ref, kn_ref, vn_ref, kt_ref, vt_ref, mp_ref, mn_ref, hm_ref, o_ref, *, tq):
    q = q_ref[...]
    hm = hm_ref[...]
    qbd = jnp.concatenate([q] * N_HEADS, axis=0) * hm
    s_past = _dot(qbd, kt_ref[...])
    s_new = _dot_nt(qbd, kn_ref[...])
    c_past, c_new = mp_ref[...], mn_ref[...]
    sm_past = jnp.where(c_past > 0.0, s_past, NEG_BIG)
    sm_new = jnp.where(c_new > 0.0, s_new, NEG_BIG)
    m = jnp.maximum(jnp.max(sm_past, axis=1, keepdims=True), jnp.max(sm_new, axis=1, keepdims=True))
    p_past = jnp.exp(sm_past - m) * c_past
    p_new = jnp.exp(sm_new - m) * c_new
    l = jnp.sum(p_past, axis=1, keepdims=True) + jnp.sum(p_new, axis=1, keepdims=True)
    o = (_dot_nt(p_past, vt_ref[...]) + _dot(p_new, vn_ref[...])) / l * hm
    out = o[0:tq]
    for h in range(1, N_HEADS):
        out = out + o[h * tq:(h + 1) * tq]
    o_ref[...] = out


def _attn_sample(q, k, qkv, row0, kt_all, vt_all, layer, batch, tq, past):
    q_pos = past + np.arange(tq)
    rep = lambda c: np.concatenate([c] * N_HEADS, axis=0)
    mp = rep(_multiplicity(q_pos, np.arange(past)))
    mn = rep(_multiplicity(q_pos, q_pos))
    row_head = np.arange(N_HEADS * tq) // tq
    hm = (row_head[:, None] == (np.arange(D_ATT) // HEAD_DIM)[None, :]).astype(np.float32)
    rb0 = row0 // tq
    new_spec = pl.BlockSpec((tq, D_ATT), lambda b: (b, 0))
    cache_spec = pl.BlockSpec((None, None, D_ATT, past), lambda b: (layer, b, 0, 0))
    full_spec = lambda a: pl.BlockSpec(a.shape, lambda b: (0,) * a.ndim)
    mp, mn, hm = jnp.asarray(mp), jnp.asarray(mn), jnp.asarray(hm)
    return pl.pallas_call(
        functools.partial(_attn_sample_kernel, tq=tq),
        grid=(batch,),
        in_specs=[new_spec, new_spec, pl.BlockSpec((tq, D_ATT), lambda b: (rb0 + b, 2)),
                  cache_spec, cache_spec, full_spec(mp), full_spec(mn), full_spec(hm)],
        out_specs=new_spec,
        out_shape=jax.ShapeDtypeStruct((batch * tq, D_ATT), F32),
        compiler_params=_cparams(("parallel",)),
        name="attn_sample",
    )(q, k, qkv, kt_all, vt_all, mp, mn, hm)


def _rglru_kernel(x_ref, g_ref, tail0_ref, h0_ref, cw_ref, cb_ref, wa_ref, ba_ref, wx_ref, bx_ref, lam_ref,
                  y_ref, tail_ref, hl_ref, xpad, a_s, b_s, hc, *, batch, tt):
    rows = tt * batch
    pad = (CONV_W - 1) * batch
    i = pl.program_id(0)

    @pl.when(i == 0)
    def _():
        xpad[0:pad, :] = tail0_ref[...]
        hc[...] = h0_ref[...]

    @pl.when(i > 0)
    def _():
        xpad[0:pad, :] = xpad[rows:rows + pad, :]

    xpad[pad:pad + rows, :] = x_ref[...]
    xc = cb_ref[...] + cw_ref[0:1, :] * xpad[0:rows, :]
    for j in range(1, CONV_W):
        xc = xc + cw_ref[j:j + 1, :] * xpad[j * batch:j * batch + rows, :]
    r = jax.nn.sigmoid(_dot(xc, wa_ref[...]) + ba_ref[...])
    ig = jax.nn.sigmoid(_dot(xc, wx_ref[...]) + bx_ref[...])
    log_a = (-LRU_C) * r * _softplus(-lam_ref[...])
    a = jnp.exp(log_a)
    a_s[...] = a
    b_s[...] = jnp.sqrt(-jnp.tanh(log_a) * (a * a + 1.0)) * (ig * xc)

    def step(t, h):
        sl = pl.ds(pl.multiple_of(t * batch, batch), batch)
        h = a_s[sl, :] * h + b_s[sl, :]
        a_s[sl, :] = h
        return h

    h = lax.fori_loop(0, tt, step, hc[...])
    hc[...] = h
    y_ref[...] = a_s[...] * jax.nn.gelu(g_ref[...])
    tail_ref[...] = xpad[rows:rows + pad, :]
    hl_ref[...] = h


def _block_diag(w):
    nb, n, _ = w.shape
    eye = jnp.eye(nb, dtype=w.dtype)
    return (w[:, :, None, :] * eye[:, None, :, None]).reshape(nb * n, nb * n)


def _rglru(rnn_tm, tail0, h0, p, batch, seq, tt):
    rows = tt * batch
    pad = (CONV_W - 1) * batch
    row_spec = pl.BlockSpec((rows, D_ATT), lambda i: (i, 0))
    const = lambda shape: pl.BlockSpec(shape, lambda i: (0,) * len(shape))
    vec = const((1, D_ATT))
    x_tm = g_tm = rnn_tm
    return pl.pallas_call(
        functools.partial(_rglru_kernel, batch=batch, tt=tt),
        grid=(seq // tt,),
        in_specs=[row_spec, pl.BlockSpec((rows, D_ATT), lambda i: (i, 1)), const((pad, D_ATT)),
                  const((batch, D_ATT)), const((CONV_W, D_ATT)), vec,
                  const((D_ATT, D_ATT)), vec, const((D_ATT, D_ATT)), vec, vec],
        out_specs=[row_spec, const((pad, D_ATT)), const((batch, D_ATT))],
        out_shape=[jax.ShapeDtypeStruct((seq * batch, D_ATT), F32),
                   jax.ShapeDtypeStruct((pad, D_ATT), F32),
                   jax.ShapeDtypeStruct((batch, D_ATT), F32)],
        scratch_shapes=[pltpu.VMEM((pad + rows, D_ATT), F32), pltpu.VMEM((rows, D_ATT), F32),
                        pltpu.VMEM((rows, D_ATT), F32), pltpu.VMEM((batch, D_ATT), F32)],
        compiler_params=_cparams(("arbitrary",)),
        name="rglru",
    )(x_tm, g_tm, tail0, h0, p["conv_w"], p["conv_b"].reshape(1, D_ATT),
      _block_diag(p["rg_wa"]).astype(BF16), p["rg_ba"].reshape(1, D_ATT),
      _block_diag(p["rg_wx"]).astype(BF16), p["rg_bx"].reshape(1, D_ATT), p["rg_lambda"].reshape(1, D_ATT))


def _rwkv_token_math(c, prev, mu, w0, w2, a0, a2, g2, kk_w, ka_w, ones):
    cm = c + (prev - c) * mu
    r = cm[:, 0:D_ATT]
    k = cm[:, D_ATT:2 * D_ATT]
    v = cm[:, 2 * D_ATT:3 * D_ATT]
    lo = cm[:, 3 * D_ATT:3 * D_ATT + LORA_W + LORA_A]
    g_lo = cm[:, 3 * D_ATT + LORA_W + LORA_A:]
    w = -_softplus(-(w0 + _dot(jnp.tanh(lo), w2))) - 0.5
    a = jax.nn.sigmoid(a0 + _dot(lo, a2))
    g = _dot(jax.nn.sigmoid(g_lo), g2)
    kk = k * kk_w
    nrm = jnp.sqrt(_dot_split(kk * kk, ones, 2))
    return r, k * (1.0 + (a - 1.0) * ka_w), v, kk / jnp.maximum(nrm, 1e-12), a, -jnp.exp(w), g


def _rwkv_pre_kernel(c_ref, prev_ref, mu_ref, w0_ref, w2_ref, a0_ref, a2_ref, g2_ref, kk_ref, ka_ref, ones_ref,
                     *out_refs):
    vals = _rwkv_token_math(c_ref[...], prev_ref[...], mu_ref[...], w0_ref[...], w2_ref[...], a0_ref[...],
                            a2_ref[...], g2_ref[...], kk_ref[...], ka_ref[...], ones_ref[...])
    for ref, val in zip(out_refs, vals):
        ref[...] = val


def _rwkv_token_params(p):
    w2p = jnp.concatenate([p["rwkv_w2"], jnp.zeros((LORA_A, D_ATT), F32)], axis=0).astype(BF16)
    a2p = jnp.concatenate([jnp.zeros((LORA_W, D_ATT), F32), p["rwkv_a2"]], axis=0).astype(BF16)
    return (p["rwkv_mu"].reshape(1, C_COLS), p["rwkv_w0"].reshape(1, D_ATT), w2p, p["rwkv_a0"].reshape(1, D_ATT),
            a2p, p["rwkv_g2"].astype(BF16), p["rwkv_kk"].reshape(1, D_ATT), p["rwkv_ka"].reshape(1, D_ATT),
            _head_ones())


def _rwkv_pre(cc, prev, p, tm=256):
    n = cc.shape[0]
    params = _rwkv_token_params(p)
    row = pl.BlockSpec((tm, C_COLS), lambda i: (i, 0))
    const = lambda a: pl.BlockSpec(a.shape, lambda i: (0,) * a.ndim)
    out = pl.BlockSpec((tm, D_ATT), lambda i: (i, 0))
    return pl.pallas_call(
        _rwkv_pre_kernel,
        grid=(n // tm,),
        in_specs=[row, row] + [const(a) for a in params],
        out_specs=[out] * 7,
        out_shape=[jax.ShapeDtypeStruct((n, D_ATT), F32)] * 7,
        compiler_params=_cparams(("parallel",)),
        name="rwkv_pre",
    )(cc, prev, *params)


def _wkv_kernel(*refs, c, bb):
    c_refs = refs[:bb]
    (shift0_ref, s0_ref, mu_ref, w0_ref, w2_ref, a0_ref, a2_ref, g2_ref, kk_ref, ka_ref, ones_ref,
     rk_ref, lnw_ref, lnb_ref, tri_ref, sl_ref, il_ref, eye_ref, bm_ref,
     o_ref, sl_out_ref, s_scr, y_scr, last_scr) = refs[bb:]
    ci = pl.program_id(1)

    @pl.when(ci == 0)
    def _():
        s_scr[...] = s0_ref[...]
        last_scr[...] = shift0_ref[...]

    rk, lnw, lnb, tri, strict, incl, eye, bmask, ones = (
        rk_ref[...], lnw_ref[...], lnb_ref[...], tri_ref[...], sl_ref[...], il_ref[...], eye_ref[...],
        bm_ref[...], ones_ref[...])
    token_params = (mu_ref[...], w0_ref[...], w2_ref[...], a0_ref[...], a2_ref[...], g2_ref[...], kk_ref[...],
                    ka_ref[...], ones)
    first_row = lax.broadcasted_iota(jnp.int32, (c, 1), 0) == 0
    lane = lax.broadcasted_iota(jnp.int32, (1, LANES), 1)
    first = lane < HEAD_DIM
    stack = lambda z: jnp.concatenate([jnp.where(first, z, 0.0), jnp.where(first, 0.0, z)], axis=0)
    twice = lambda z: jnp.concatenate([z, z], axis=0)
    fold = lambda z: z[:c] + z[c:]
    n_fac = int(np.log2(c))
    two_c = 2 * c

    seqs = []
    for bi in range(bb):
        cols = c_refs[bi][0]
        prev = jnp.where(first_row, last_scr[bi], pltpu.roll(cols, 1, 0))
        last_scr[bi] = cols[c - 1:c, :]
        r, k, v, kap, a, lw, g = _rwkv_token_math(cols, prev, *token_params)
        cum = _split_dot_left(tri, lw, 3)
        tot = cum[c - 1:c, :]
        e_neg, e_end = jnp.exp(-cum), jnp.exp(tot - cum)
        b = kap * a
        seqs.append(dict(r=r, k=k, v=v, g=g, kap_t=kap * jnp.exp(cum - lw), r_t=r * jnp.exp(cum), k_t=k * e_neg,
                         b_t=b * e_neg, k_h=k * e_end, b_h=b * e_end, g_end=jnp.exp(tot)))
    chains = [(bi, hp, slice(hp * LANES, (hp + 1) * LANES)) for bi in range(bb) for hp in range(HEAD_PAIRS)]

    grams, states, v_stacks = [], [], []
    for bi, hp, sl in chains:
        q = seqs[bi]
        grams.append(_dot_nt(jnp.concatenate([stack(q["kap_t"][:, sl]), stack(q["r_t"][:, sl])], axis=0),
                             jnp.concatenate([twice(q["k_t"][:, sl]), twice(q["b_t"][:, sl])], axis=0)))
        states.append(s_scr[bi, hp])
        v_stacks.append(stack(q["v"][:, sl]))
    from_state = [_dot_nt(jnp.concatenate([seqs[bi]["kap_t"][:, sl], seqs[bi]["r_t"][:, sl]], axis=0), s_p)
                  for (bi, hp, sl), s_p in zip(chains, states)]
    a_kk = [gm[:two_c, :two_c] * strict for gm in grams]
    y_mat = [jnp.concatenate([gm[two_c:, :two_c] * incl, -(gm[two_c:, two_c:] * incl)], axis=1) for gm in grams]
    pw = [-(gm[:two_c, two_c:] * strict) for gm in grams]
    inv = [eye + m for m in pw]
    for _ in range(n_fac - 1):
        pw = [_dot(m, m) for m in pw]
        inv = [t + _dot(t, m) for t, m in zip(inv, pw)]
    x_s = [stack(fs[:c]) + _dot(am, vs) for fs, am, vs in zip(from_state, a_kk, v_stacks)]
    u_s = [_dot(t, x) for t, x in zip(inv, x_s)]
    y_s = [stack(fs[c:]) + _dot(ym, jnp.concatenate([vs, u], axis=0))
           for fs, ym, vs, u in zip(from_state, y_mat, v_stacks, u_s)]
    upd = [_dot(jnp.concatenate([seqs[bi]["v"][:, sl], fold(u)], axis=0).T,
                jnp.concatenate([seqs[bi]["k_h"][:, sl], -seqs[bi]["b_h"][:, sl]], axis=0))
           for (bi, hp, sl), u in zip(chains, u_s)]
    for (bi, hp, sl), s_p, d_s, y in zip(chains, states, upd, y_s):
        s_scr[bi, hp] = s_p * seqs[bi]["g_end"][:, sl] + bmask * d_s
        y_scr[bi, :, sl] = fold(y)

    inv_n = 1.0 / HEAD_DIM
    for bi in range(bb):
        q = seqs[bi]
        y = y_scr[bi]
        mean = _dot_split(y, ones, 3) * inv_n
        d = y - mean
        var = _dot_split(d * d, ones, 3) * inv_n
        yn = d * lax.rsqrt(var + GN_EPS) * lnw + lnb
        bonus = _dot_split(q["r"] * q["k"] * rk, ones, 3) * q["v"]
        o_ref[bi, 0] = (yn + bonus) * q["g"]

    @pl.when(ci == pl.num_programs(1) - 1)
    def _():
        sl_out_ref[...] = s_scr[...]


def _wkv(cc, shift0, s0_pairs, p, batch, seq, c=64, bb=4):
    assert 2 * c == LANES
    nc = seq // c
    cc3 = cc.reshape(batch * nc, c, C_COLS)
    token_params = _rwkv_token_params(p)
    idx = np.arange(2 * c)
    same = (idx[:, None] // c) == (idx[None, :] // c)
    strict = (same & (idx[None, :] % c < idx[:, None] % c)).astype(np.float32)
    incl = (same & (idx[None, :] % c <= idx[:, None] % c)).astype(np.float32)
    eye = np.eye(2 * c, dtype=np.float32)
    half = np.arange(LANES) // HEAD_DIM
    bmask = (half[:, None] == half[None, :]).astype(np.float32)
    tri = (np.arange(c)[None, :] <= np.arange(c)[:, None]).astype(np.float32)
    def chunk(j):
        return pl.BlockSpec((1, c, C_COLS), lambda g, i: ((g * bb + j) * nc + i, 0, 0))

    const = lambda a: pl.BlockSpec(a.shape, lambda g, i: (0,) * a.ndim)
    state = pl.BlockSpec((bb, HEAD_PAIRS, LANES, LANES), lambda g, i: (g, 0, 0, 0))
    consts = token_params + (p["rwkv_rk"].reshape(1, D_ATT), p["rwkv_lnw"].reshape(1, D_ATT),
                             p["rwkv_lnb"].reshape(1, D_ATT), jnp.asarray(tri, dtype=BF16), jnp.asarray(strict),
                             jnp.asarray(incl), jnp.asarray(eye), jnp.asarray(bmask))
    return pl.pallas_call(
        functools.partial(_wkv_kernel, c=c, bb=bb),
        grid=(batch // bb, nc),
        in_specs=[chunk(j) for j in range(bb)]
        + [pl.BlockSpec((bb, 1, C_COLS), lambda g, i: (g, 0, 0)), state] + [const(a) for a in consts],
        out_specs=[pl.BlockSpec((bb, 1, c, D_ATT), lambda g, i: (g, i, 0, 0)), state],
        out_shape=[jax.ShapeDtypeStruct((batch, nc, c, D_ATT), F32),
                   jax.ShapeDtypeStruct((batch, HEAD_PAIRS, LANES, LANES), F32)],
        scratch_shapes=[pltpu.VMEM((bb, HEAD_PAIRS, LANES, LANES), F32), pltpu.VMEM((bb, c, D_ATT), F32),
                        pltpu.VMEM((bb, 1, C_COLS), F32)],
        compiler_params=_cparams(("parallel", "arbitrary")),
        name="wkv",
    )(*([cc3] * bb), shift0.reshape(batch, 1, C_COLS), s0_pairs, *consts)


def _wkv_lane_kernel(r_ref, k_ref, v_ref, kap_ref, a_ref, lw_ref, g_ref, s0_ref, rk_ref, lnw_ref, lnb_ref,
                     o_ref, s_ref, *, steps):
    s_ref[...] = s0_ref[...]
    inv_n = 1.0 / HEAD_DIM
    for t in range(steps):
        kap = kap_ref[t]
        w = jnp.exp(lw_ref[t])
        bt = kap * a_ref[t]
        kt = k_ref[t]
        rt = r_ref[t]

        def per_v(vi, _, t=t, kap=kap, w=w, bt=bt, kt=kt, rt=rt):
            s = s_ref[vi]
            sa = -jnp.sum(s * kap, axis=0, keepdims=True)
            vv = v_ref[t, pl.ds(vi, 1), :]
            s = s * w + sa * bt + vv * kt
            s_ref[vi] = s
            o_ref[t, pl.ds(vi, 1), :] = jnp.sum(s * rt, axis=0, keepdims=True)
            return 0

        lax.fori_loop(0, HEAD_DIM, per_v, 0, unroll=4)
        y = o_ref[t]
        mean = jnp.sum(y, axis=0, keepdims=True) * inv_n
        d = y - mean
        var = jnp.sum(d * d, axis=0, keepdims=True) * inv_n
        yn = d * lax.rsqrt(var + GN_EPS) * lnw_ref[...] + lnb_ref[...]
        bonus = jnp.sum(rt * kt * rk_ref[...], axis=0, keepdims=True) * v_ref[t]
        o_ref[t] = (yn + bonus) * g_ref[t]


def _wkv_lane(parts_t, s0_t, p, batch, steps):
    tok = pl.BlockSpec((steps, HEAD_DIM, batch), lambda h: (0, h, 0))
    st = pl.BlockSpec((HEAD_DIM, HEAD_DIM, batch), lambda h: (h, 0, 0))
    col = pl.BlockSpec((HEAD_DIM, batch), lambda h: (h, 0))
    bcast = lambda z: jnp.broadcast_to(z.reshape(D_ATT, 1), (D_ATT, batch))
    return pl.pallas_call(
        functools.partial(_wkv_lane_kernel, steps=steps),
        grid=(N_HEADS,),
        in_specs=[tok] * 7 + [st, col, col, col],
        out_specs=[tok, st],
        out_shape=[jax.ShapeDtypeStruct((steps, D_ATT, batch), F32),
                   jax.ShapeDtypeStruct((D_ATT, HEAD_DIM, batch), F32)],
        compiler_params=_cparams(("parallel",)),
        name="wkv_lane",
    )(*parts_t, s0_t, bcast(p["rwkv_rk"]), bcast(p["rwkv_lnw"]), bcast(p["rwkv_lnb"]))


def _pair_states(s):
    b = s.shape[0]
    s = s.reshape(b, HEAD_PAIRS, 2, HEAD_DIM, HEAD_DIM)
    z = jnp.zeros((b, HEAD_PAIRS, HEAD_DIM, HEAD_DIM), s.dtype)
    top = jnp.concatenate([s[:, :, 0], z], axis=-1)
    bot = jnp.concatenate([z, s[:, :, 1]], axis=-1)
    return jnp.concatenate([top, bot], axis=-2)


def _unpair_states(sp):
    b = sp.shape[0]
    first = sp[:, :, :HEAD_DIM, :HEAD_DIM]
    second = sp[:, :, HEAD_DIM:, HEAD_DIM:]
    return jnp.stack([first, second], axis=2).reshape(b, N_HEADS, HEAD_DIM, HEAD_DIM)


def _merge_kernel(oa_ref, ob_ref, oc_ref, gl_ref, x_ref, bm_ref, wa_ref, wb_ref, wc_ref, wo_ref, nw_ref,
                  wgh_ref, wgl_ref, bg_ref, x1_ref, xn_ref, comb_ref):
    merged = None
    for j, (o_ref, w_ref) in enumerate(((oa_ref, wa_ref), (ob_ref, wb_ref), (oc_ref, wc_ref))):
        cols = slice(j * D_MODEL, (j + 1) * D_MODEL)
        gate = jax.nn.sigmoid(gl_ref[:, cols] + bm_ref[:, cols])
        t = gate * _dot(o_ref[...], w_ref[...])
        merged = t if merged is None else merged + t
    x1 = x_ref[...] + _dot(merged, wo_ref[...])
    x1_ref[...] = x1
    xn = x1 * lax.rsqrt(jnp.mean(x1 * x1, axis=-1, keepdims=True) + EPS) * nw_ref[...]
    xn_ref[...] = xn.astype(BF16)
    hi = xn.astype(BF16)
    lo = (xn - hi.astype(F32)).astype(BF16)
    wgh, wgl = wgh_ref[...], wgl_ref[...]
    lg = (jnp.dot(hi, wgh, preferred_element_type=F32) + jnp.dot(lo, wgh, preferred_element_type=F32)
          + jnp.dot(hi, wgl, preferred_element_type=F32)) + bg_ref[...]
    col = lambda j: lg[:, j:j + 1]
    best = col(0)
    gi = jnp.zeros(best.shape, jnp.int32)
    for j in range(1, N_GROUPS):
        better = col(j) > best
        best = jnp.where(better, col(j), best)
        gi = jnp.where(better, j, gi)
    den = jnp.zeros(best.shape, F32)
    for j in range(N_GROUPS):
        den = den + jnp.exp(col(j) - best)
    p_group = 1.0 / den
    le = []
    for j in range(EXP_PER_GROUP):
        z = col(N_GROUPS + j)
        for grp in range(1, N_GROUPS):
            z = jnp.where(gi == grp, col(N_GROUPS + grp * EXP_PER_GROUP + j), z)
        le.append(z)
    v1 = le[0]
    i1 = jnp.zeros(best.shape, jnp.int32)
    for j in range(1, EXP_PER_GROUP):
        better = le[j] > v1
        v1 = jnp.where(better, le[j], v1)
        i1 = jnp.where(better, j, i1)
    v2 = jnp.full(best.shape, -jnp.inf, F32)
    i2 = jnp.zeros(best.shape, jnp.int32)
    for j in range(EXP_PER_GROUP):
        cand = jnp.where(i1 == j, -jnp.inf, le[j])
        better = cand > v2
        v2 = jnp.where(better, cand, v2)
        i2 = jnp.where(better, j, i2)
    e2 = jnp.exp(v2 - v1)
    w1 = p_group / (1.0 + e2)
    w2 = p_group * e2 / (1.0 + e2)
    lane = lax.broadcasted_iota(jnp.int32, lg.shape, 1)
    base = gi * EXP_PER_GROUP
    comb_ref[...] = jnp.where(lane == base + i1, w1, 0.0) + jnp.where(lane == base + i2, w2, 0.0)


def _merge(oa, ob, oc, gl, x, p, wb, tm=256, ob_time_major=None):
    n = x.shape[0]
    ob_spec = pl.BlockSpec((tm, D_ATT), lambda i: (i, 0))
    if ob_time_major is not None:
        batch, seq = ob_time_major
        nt = seq // tm
        ob = ob.reshape(seq, batch * D_ATT)
        ob_spec = pl.BlockSpec((tm, D_ATT), lambda i: (i % nt, i // nt))
    wg = jnp.concatenate([p["moe_wg_group"], p["moe_wg_exp"]], axis=1)
    wg = jnp.pad(wg, ((0, 0), (0, LANES - wg.shape[1])))
    wgh = wg.astype(BF16)
    wgl = (wg - wgh.astype(F32)).astype(BF16)
    bg = jnp.pad(jnp.concatenate([p["moe_bg_group"], p["moe_bg_exp"]]), (0, LANES - N_GROUPS - N_EXPERTS))
    row = lambda w: pl.BlockSpec((tm, w), lambda i: (i, 0))
    const = lambda shape: pl.BlockSpec(shape, lambda i: (0,) * len(shape))
    return pl.pallas_call(
        _merge_kernel,
        grid=(n // tm,),
        in_specs=[row(D_ATT), ob_spec, row(D_ATT), row(GL_COLS), row(D_MODEL), const((1, GL_COLS)),
                  const((D_ATT, D_MODEL)), const((D_ATT, D_MODEL)), const((D_ATT, D_MODEL)),
                  const((D_MODEL, D_MODEL)), const((1, D_MODEL)), const((D_MODEL, LANES)),
                  const((D_MODEL, LANES)), const((1, LANES))],
        out_specs=[row(D_MODEL), row(D_MODEL), row(LANES)],
        out_shape=[jax.ShapeDtypeStruct((n, D_MODEL), F32), jax.ShapeDtypeStruct((n, D_MODEL), BF16),
                   jax.ShapeDtypeStruct((n, LANES), F32)],
        compiler_params=_cparams(("parallel",)),
        name="merge",
    )(oa, ob, oc, gl, x, p["b_merge"].reshape(1, GL_COLS), wb["w_branch_a"], wb["w_branch_b"], wb["w_branch_c"],
      wb["w_out"], p["norm2_w"].reshape(1, D_MODEL), wgh, wgl, bg.reshape(1, LANES))


def _moe_kernel(xn_ref, comb_ref, x1_ref, w1_ref, w3_ref, w2_ref, o_ref, acc):
    e = pl.program_id(1)

    @pl.when(e == 0)
    def _():
        acc[...] = x1_ref[...]

    xb = xn_ref[...]
    lane = lax.broadcasted_iota(jnp.int32, comb_ref.shape, 1)
    wt = jnp.sum(jnp.where(lane == e, comb_ref[...], 0.0), axis=1, keepdims=True)
    h = jax.nn.silu(jnp.dot(xb, w1_ref[0], preferred_element_type=F32)) * jnp.dot(xb, w3_ref[0],
                                                                                  preferred_element_type=F32)
    acc[...] += _dot(h * wt, w2_ref[0])

    @pl.when(e == pl.num_programs(1) - 1)
    def _():
        o_ref[...] = acc[...]


def _moe(xn, comb, x1, w1, w3, w2, tm=1024):
    n = xn.shape[0]
    return pl.pallas_call(
        _moe_kernel,
        grid=(n // tm, N_EXPERTS),
        in_specs=[pl.BlockSpec((tm, D_MODEL), lambda i, e: (i, 0)),
                  pl.BlockSpec((tm, LANES), lambda i, e: (i, 0)),
                  pl.BlockSpec((tm, D_MODEL), lambda i, e: (i, 0)),
                  pl.BlockSpec((1, D_MODEL, D_EXPERT), lambda i, e: (e, 0, 0)),
                  pl.BlockSpec((1, D_MODEL, D_EXPERT), lambda i, e: (e, 0, 0)),
                  pl.BlockSpec((1, D_EXPERT, D_MODEL), lambda i, e: (e, 0, 0))],
        out_specs=pl.BlockSpec((tm, D_MODEL), lambda i, e: (i, 0)),
        out_shape=jax.ShapeDtypeStruct((n, D_MODEL), F32),
        scratch_shapes=[pltpu.VMEM((tm, D_MODEL), F32)],
        compiler_params=_cparams(("parallel", "arbitrary")),
        name="moe",
    )(xn, comb, x1, w1, w3, w2)


def _time_major(z, batch, seq):
    return z.reshape(batch, seq, z.shape[-1]).transpose(1, 0, 2).reshape(seq * batch, z.shape[-1])


def _batch_major(z, batch, seq):
    return z.reshape(seq, batch, z.shape[-1]).transpose(1, 0, 2).reshape(batch * seq, z.shape[-1])


def _group_layer(x, g, p, wb, caches, layer, tabs):
    batch, seq = g["batch"], g["seq"]
    rows = batch * seq
    prompt = g["past"] == 0
    tm = 256
    qkv, rnn, cc, gl = _inproj(x, p["norm1_w"], wb["w_in"], time_major_rnn=(batch, seq) if prompt else None)
    heads = lambda z: z.reshape(batch, seq, N_HEADS, HEAD_DIM)
    tail0 = g["conv"].transpose(1, 0, 2).reshape((CONV_W - 1) * batch, D_ATT)
    if prompt:
        q, k, kt, vt = _qkprep(qkv, 0, rows, tabs, g["table_blocks"], p["q_norm_w"], p["k_norm_w"], tm,
                               cache_batch=batch)
        oa = _attn_prompt(q, k, qkv, batch, seq)
        from_t = lambda z: z.reshape(batch, N_HEADS, HEAD_DIM, seq).transpose(0, 3, 1, 2)
        k_new, v_new = from_t(kt), from_t(vt)
        ob, tail, h_last = _rglru(rnn, tail0, g["h"], p, batch, seq, g["tt"])
        o_wkv, s_pairs = _wkv(cc, g["shift"], _pair_states(g["wkv"]), p, batch, seq)
        oc = o_wkv.reshape(rows, D_ATT)
        s_last = _unpair_states(s_pairs)
    else:
        q, k = _qkprep(qkv, 0, rows, tabs, g["table_blocks"], p["q_norm_w"], p["k_norm_w"], tm)
        oa = _attn_sample(q, k, qkv, 0, caches[0], caches[1], layer, batch, seq, g["past"])
        k_new, v_new = heads(k), heads(qkv[:, 2 * D_ATT:])
        y_tm, tail, h_last = _rglru(_time_major(rnn, batch, seq), tail0, g["h"], p, batch, seq, g["tt"])
        ob = _batch_major(y_tm, batch, seq)
        c3 = cc.reshape(batch, seq, C_COLS)
        prev = jnp.concatenate([g["shift"][:, None, :], c3[:, :-1]], axis=1).reshape(rows, C_COLS)
        to_lanes = lambda z: z.reshape(batch, seq, D_ATT).transpose(1, 2, 0)
        o_t, s_t = _wkv_lane([to_lanes(z) for z in _rwkv_pre(cc, prev, p)], g["wkv"], p, batch, seq)
        oc = o_t.transpose(2, 0, 1).reshape(rows, D_ATT)
        s_last = s_t.reshape(N_HEADS, HEAD_DIM, HEAD_DIM, batch).transpose(3, 0, 1, 2)
    conv_tail = tail.reshape(CONV_W - 1, batch, D_ATT).transpose(1, 0, 2)
    shift_last = cc.reshape(batch, seq, C_COLS)[:, -1]
    x1, xn2, comb = _merge(oa, ob, oc, gl, x, p, wb, ob_time_major=(batch, seq) if prompt else None)
    x2 = _moe(xn2, comb, x1, wb["moe_w1"], wb["moe_w3"], wb["moe_w2"], tm=min(1024, rows))
    return x2, (k_new, v_new, conv_tail, h_last, shift_last, s_last)


def kernel(x_prompt, x_sample, cache_k_win, cache_v_win, state_conv, state_h, state_shift, state_wkv, norm1_w, w_in, q_norm_w, k_norm_w, conv_w, conv_b, rg_wa, rg_ba, rg_wx, rg_bx, rg_lambda, rwkv_mu, rwkv_w0, rwkv_w2, rwkv_a0, rwkv_a2, rwkv_g2, rwkv_kk, rwkv_ka, rwkv_rk, rwkv_lnw, rwkv_lnb, b_merge, w_branch_a, w_branch_b, w_branch_c, w_out, norm2_w, moe_wg_group, moe_bg_group, moe_wg_exp, moe_bg_exp, moe_w1, moe_w3, moe_w2):
    weights = dict(norm1_w=norm1_w, w_in=w_in, q_norm_w=q_norm_w, k_norm_w=k_norm_w, conv_w=conv_w, conv_b=conv_b,
                   rg_wa=rg_wa, rg_ba=rg_ba, rg_wx=rg_wx, rg_bx=rg_bx, rg_lambda=rg_lambda, rwkv_mu=rwkv_mu,
                   rwkv_w0=rwkv_w0, rwkv_w2=rwkv_w2, rwkv_a0=rwkv_a0, rwkv_a2=rwkv_a2, rwkv_g2=rwkv_g2,
                   rwkv_kk=rwkv_kk, rwkv_ka=rwkv_ka, rwkv_rk=rwkv_rk, rwkv_lnw=rwkv_lnw, rwkv_lnb=rwkv_lnb,
                   b_merge=b_merge, w_branch_a=w_branch_a, w_branch_b=w_branch_b, w_branch_c=w_branch_c,
                   w_out=w_out, norm2_w=norm2_w, moe_wg_group=moe_wg_group, moe_bg_group=moe_bg_group,
                   moe_wg_exp=moe_wg_exp, moe_bg_exp=moe_bg_exp, moe_w1=moe_w1, moe_w3=moe_w3, moe_w2=moe_w2)
    depth = w_in.shape[0]
    bp, tp = x_prompt.shape[0], x_prompt.shape[1]
    bs, ts = x_sample.shape[0], x_sample.shape[1]
    past = cache_k_win.shape[2]
    x_p = x_prompt.reshape(bp * tp, D_MODEL)
    x_s = x_sample.reshape(bs * ts, D_MODEL)
    tm = 256
    tab_p = _rope_tables(jnp.arange(tp, dtype=jnp.int32))
    tab_s = tuple(jnp.tile(t, (tm // ts, 1)) for t in _rope_tables(past + jnp.arange(ts, dtype=jnp.int32)))
    kt_all = cache_k_win.transpose(0, 1, 3, 4, 2).reshape(depth, bs, D_ATT, past)
    vt_all = cache_v_win.transpose(0, 1, 3, 4, 2).reshape(depth, bs, D_ATT, past)
    wkv_t = state_wkv.transpose(0, 2, 3, 4, 1).reshape(depth, D_ATT, HEAD_DIM, bs)
    outs_p = [[] for _ in range(6)]
    outs_s = [[] for _ in range(6)]
    big = ("w_in", "w_branch_a", "w_branch_b", "w_branch_c", "w_out", "moe_w1", "moe_w3", "moe_w2")
    for l in range(depth):
        p = {k_: v_[l] for k_, v_ in weights.items()}
        wb = {k_: p[k_].astype(BF16) for k_ in big}
        g_p = dict(batch=bp, seq=tp, past=0, table_blocks=tp // tm, tt=64,
                   conv=jnp.zeros((bp, CONV_W - 1, D_ATT), F32), h=jnp.zeros((bp, D_ATT), F32),
                   shift=jnp.zeros((bp, C_COLS), F32), wkv=jnp.zeros((bp, N_HEADS, HEAD_DIM, HEAD_DIM), F32))
        g_s = dict(batch=bs, seq=ts, past=past, table_blocks=1, tt=ts,
                   conv=state_conv[l], h=state_h[l], shift=state_shift[l], wkv=wkv_t[l])
        x_p, st_p = _group_layer(x_p, g_p, p, wb, None, l, tab_p)
        x_s, st_s = _group_layer(x_s, g_s, p, wb, (kt_all, vt_all), l, tab_s)
        for j in range(6):
            outs_p[j].append(st_p[j])
            outs_s[j].append(st_s[j])
    y_prompt = x_p.reshape(bp, tp, D_MODEL)
    y_sample = x_s.reshape(bs, ts, D_MODEL)
    stack = lambda zs: jnp.stack(zs)
    return (y_prompt, y_sample, *[stack(z) for z in outs_p], *[stack(z) for z in outs_s])
```

```python
import functools

import numpy as np
import jax
import jax.numpy as jnp
from jax import lax
from jax.experimental import pallas as pl
from jax.experimental.pallas import tpu as pltpu

F32 = jnp.float32
BF16 = jnp.bfloat16

D_MODEL = 1024
N_HEADS = 8
HEAD_DIM = 64
D_ATT = N_HEADS * HEAD_DIM
ROT_DIM = HEAD_DIM // 4
ROPE_THETA = 500000.0
PATTERNS = ((128, 1), (512, 4), (2048, 16))
CONV_W = 4
LRU_C = 8.0
LORA_W, LORA_A, LORA_G = 64, 64, 128
C_COLS = 3 * D_ATT + LORA_W + LORA_A + LORA_G
N_BRANCH = 3
GL_COLS = N_BRANCH * D_MODEL
N_GROUPS, EXP_PER_GROUP, N_EXPERTS = 4, 4, 16
D_EXPERT = 512
EPS = 1e-6
GN_EPS = 64e-5

LANES = 128
SUBLANES = 8
VMEM_LIMIT = 56 * 1024 * 1024
HEAD_PAIRS = D_ATT // LANES
NEG_BIG = -1e30


def _cparams(sem):
    return pltpu.CompilerParams(dimension_semantics=sem, vmem_limit_bytes=VMEM_LIMIT)


def _dot(a, b):
    return jnp.dot(a.astype(BF16), b.astype(BF16), preferred_element_type=F32)


def _dot_nt(a, b):
    return lax.dot_general(a.astype(BF16), b.astype(BF16), (((1,), (1,)), ((), ())),
                           preferred_element_type=F32)


def _dot_split(x, w_bf16, passes):
    acc = None
    r = x
    for i in range(passes):
        p = r.astype(BF16)
        t = jnp.dot(p, w_bf16, preferred_element_type=F32)
        acc = t if acc is None else acc + t
        if i + 1 < passes:
            r = r - p.astype(F32)
    return acc


def _split_dot_left(w_exact_bf16, x, passes):
    acc = None
    r = x
    for i in range(passes):
        p = r.astype(BF16)
        t = jnp.dot(w_exact_bf16, p, preferred_element_type=F32)
        acc = t if acc is None else acc + t
        if i + 1 < passes:
            r = r - p.astype(F32)
    return acc


def _softplus(x):
    return jnp.maximum(x, 0.0) + jnp.log1p(jnp.exp(-jnp.abs(x)))


def _head_ones():
    h = np.arange(D_ATT) // HEAD_DIM
    return jnp.asarray((h[:, None] == h[None, :]).astype(np.float32), dtype=BF16)


def _inproj_kernel(x_ref, nw_ref, w_ref, qkv_ref, rg_ref, cc_ref, gl_ref):
    x = x_ref[...]
    xn = x * lax.rsqrt(jnp.mean(x * x, axis=-1, keepdims=True) + EPS) * nw_ref[...]
    xb = xn.astype(BF16)
    off = 0
    for ref in (qkv_ref, rg_ref, cc_ref, gl_ref):
        n = ref.shape[-1]
        ref[...] = jnp.dot(xb, w_ref[:, off:off + n], preferred_element_type=F32)
        off += n


def _inproj(x, norm_w, w_in_bf16, tm=256, time_major_rnn=None):
    n = x.shape[0]
    widths = (3 * D_ATT, 2 * D_ATT, C_COLS, GL_COLS)
    in_cols = sum(widths)
    out_specs = [pl.BlockSpec((tm, w), lambda i: (i, 0)) for w in widths]
    out_shape = [jax.ShapeDtypeStruct((n, w), F32) for w in widths]
    if time_major_rnn is not None:
        batch, seq = time_major_rnn
        nt = seq // tm
        out_specs[1] = pl.BlockSpec((tm, widths[1]), lambda i: (i % nt, i // nt))
        out_shape[1] = jax.ShapeDtypeStruct((seq, batch * widths[1]), F32)
    outs = list(pl.pallas_call(
        _inproj_kernel,
        grid=(n // tm,),
        in_specs=[pl.BlockSpec((tm, D_MODEL), lambda i: (i, 0)),
                  pl.BlockSpec((1, D_MODEL), lambda i: (0, 0)),
                  pl.BlockSpec((D_MODEL, in_cols), lambda i: (0, 0))],
        out_specs=out_specs,
        out_shape=out_shape,
        compiler_params=_cparams(("parallel",)),
        name="inproj",
    )(x, norm_w.reshape(1, D_MODEL), w_in_bf16))
    if time_major_rnn is not None:
        outs[1] = outs[1].reshape(n, widths[1])
    return outs


def _rope_tables(pos):
    half = ROT_DIM // 2
    inv = ROPE_THETA ** (-jnp.arange(half, dtype=F32) / half)
    ang = pos.astype(F32)[:, None] * inv[None, :]
    cos, sin = jnp.cos(ang), jnp.sin(ang)
    t = pos.shape[0]
    rest = HEAD_DIM - ROT_DIM
    c_head = jnp.concatenate([cos, cos, jnp.ones((t, rest), F32)], axis=1)
    s_prev = jnp.concatenate([jnp.zeros((t, half), F32), sin, jnp.zeros((t, rest), F32)], axis=1)
    s_next = jnp.concatenate([-sin, jnp.zeros((t, half), F32), jnp.zeros((t, rest), F32)], axis=1)
    tile = lambda z: jnp.tile(z, (1, N_HEADS))
    return tile(c_head), tile(s_prev), tile(s_next)


def _qkprep_kernel(q_ref, k_ref, v_ref, cos_ref, sp_ref, sn_ref, qw_ref, kw_ref, ones_ref, qo_ref, ko_ref,
                   *cache_refs):
    half = ROT_DIM // 2
    cos, sp, sn = cos_ref[...], sp_ref[...], sn_ref[...]
    for src, w_ref, dst, scale in ((q_ref, qw_ref, qo_ref, HEAD_DIM ** -0.5), (k_ref, kw_ref, ko_ref, None)):
        x = src[...]
        ms = _dot_split(x * x, ones_ref[...], 2) * (1.0 / HEAD_DIM)
        xn = x * lax.rsqrt(ms + EPS) * w_ref[...]
        y = xn * cos + pltpu.roll(xn, half, 1) * sp + pltpu.roll(xn, D_ATT - half, 1) * sn
        dst[...] = y if scale is None else y * scale
    if cache_refs:
        kt_ref, vt_ref = cache_refs
        kt_ref[...] = ko_ref[...].T
        vt_ref[...] = v_ref[...].T


def _qkprep(qkv, row0, nrows, tables, table_blocks, qw, kw, tm, cache_batch=0):
    cos, sp, sn = tables
    rb0 = row0 // tm
    tab_spec = pl.BlockSpec((tm, D_ATT), lambda i: (i % table_blocks, 0))
    vec = pl.BlockSpec((1, D_ATT), lambda i: (0, 0))
    out_specs = [pl.BlockSpec((tm, D_ATT), lambda i: (i, 0))] * 2
    out_shape = [jax.ShapeDtypeStruct((nrows, D_ATT), F32)] * 2
    if cache_batch:
        out_specs += [pl.BlockSpec((None, D_ATT, tm), lambda i: (i // table_blocks, 0, i % table_blocks))] * 2
        out_shape += [jax.ShapeDtypeStruct((cache_batch, D_ATT, table_blocks * tm), F32)] * 2
    return pl.pallas_call(
        _qkprep_kernel,
        grid=(nrows // tm,),
        in_specs=[pl.BlockSpec((tm, D_ATT), lambda i: (rb0 + i, 0)),
                  pl.BlockSpec((tm, D_ATT), lambda i: (rb0 + i, 1)),
                  pl.BlockSpec((tm, D_ATT), lambda i: (rb0 + i, 2)),
                  tab_spec, tab_spec, tab_spec, vec, vec,
                  pl.BlockSpec((D_ATT, D_ATT), lambda i: (0, 0))],
        out_specs=out_specs,
        out_shape=out_shape,
        compiler_params=_cparams(("parallel",)),
        name="qkprep",
    )(qkv, qkv, qkv, cos, sp, sn, jnp.tile(qw, N_HEADS).reshape(1, D_ATT),
      jnp.tile(kw, N_HEADS).reshape(1, D_ATT), _head_ones())


def _multiplicity(q_pos, k_pos):
    d = q_pos[:, None] - k_pos[None, :]
    c = np.zeros(d.shape, np.float32)
    for window, dil in PATTERNS:
        c += ((d >= 0) & (d <= window) & (d % dil == 0) & (k_pos[None, :] >= 0)).astype(np.float32)
    return c


def _attn_prompt_kernel(q_ref, k_ref, v_ref, mask_ref, o_ref, *, tq):
    qi = pl.program_id(1)
    lane = lax.broadcasted_iota(jnp.int32, (1, LANES), 1)
    first = lane < HEAD_DIM
    qms = []
    for hp in range(HEAD_PAIRS):
        qp = q_ref[:, hp * LANES:(hp + 1) * LANES]
        qms.append(jnp.where(first, qp, 0.0).astype(BF16))
        qms.append(jnp.where(first, 0.0, qp).astype(BF16))

    def body(kj, carry):
        rows = pl.ds(pl.multiple_of(kj * tq, tq), tq)
        c = mask_ref[qi - kj]
        valid = c > 0.0
        new = []
        pair = lambda h: slice((h // 2) * LANES, (h // 2 + 1) * LANES)
        score = lambda h: _dot_nt(qms[h], k_ref[rows, pair(h)])
        s_next = score(0)
        for h in range(N_HEADS):
            sl = pair(h)
            m, l, acc = carry[h]
            s = s_next
            if h + 1 < N_HEADS:
                s_next = score(h + 1)
            sm = jnp.where(valid, s, NEG_BIG)
            m_new = jnp.maximum(m, jnp.max(sm, axis=1, keepdims=True))
            alpha = jnp.exp(m - m_new)
            p = jnp.exp(sm - m_new) * c
            l = alpha * l + jnp.sum(p, axis=1, keepdims=True)
            acc = alpha * acc + _dot(p, v_ref[rows, sl])
            new.append((m_new, l, acc))
        return tuple(new)

    init = tuple((jnp.full((tq, 1), NEG_BIG, F32), jnp.zeros((tq, 1), F32), jnp.zeros((tq, LANES), F32))
                 for _ in range(N_HEADS))
    res = lax.fori_loop(0, qi + 1, body, init)
    for hp in range(HEAD_PAIRS):
        (_, l0, a0), (_, l1, a1) = res[2 * hp], res[2 * hp + 1]
        o_ref[:, hp * LANES:(hp + 1) * LANES] = jnp.where(first, a0 / l0, a1 / l1)


def _attn_prompt(q, k, qkv, batch, seq, tq=512):
    nb = seq // tq
    pos = np.arange(tq)
    masks = np.stack([_multiplicity(pos + d * tq, pos) for d in range(nb)])
    return pl.pallas_call(
        functools.partial(_attn_prompt_kernel, tq=tq),
        grid=(batch, nb),
        in_specs=[pl.BlockSpec((tq, D_ATT), lambda b, i: (b * nb + i, 0)),
                  pl.BlockSpec((seq, D_ATT), lambda b, i: (b, 0)),
                  pl.BlockSpec((seq, D_ATT), lambda b, i: (b, 2)),
                  pl.BlockSpec((nb, tq, tq), lambda b, i: (0, 0, 0))],
        out_specs=pl.BlockSpec((tq, D_ATT), lambda b, i: (b * nb + i, 0)),
        out_shape=jax.ShapeDtypeStruct((batch * seq, D_ATT), F32),
        compiler_params=_cparams(("parallel", "arbitrary")),
        name="attn_prompt",
    )(q, k, qkv, jnp.asarray(masks))


def _attn_sample_kernel(q_ref, kn_ref, vn_ref, kt_ref, vt_ref, mp_ref, mn_ref, hm_ref, o_ref, *, tq):
    q = q_ref[...]
    hm = hm_ref[...]
    qbd = jnp.concatenate([q] * N_HEADS, axis=0) * hm
    s_past = _dot(qbd, kt_ref[...])
    s_new = _dot_nt(qbd, kn_ref[...])
    c_past, c_new = mp_ref[...], mn_ref[...]
    sm_past = jnp.where(c_past > 0.0, s_past, NEG_BIG)
    sm_new = jnp.where(c_new > 0.0, s_new, NEG_BIG)
    m = jnp.maximum(jnp.max(sm_past, axis=1, keepdims=True), jnp.max(sm_new, axis=1, keepdims=True))
    p_past = jnp.exp(sm_past - m) * c_past
    p_new = jnp.exp(sm_new - m) * c_new
    l = jnp.sum(p_past, axis=1, keepdims=True) + jnp.sum(p_new, axis=1, keepdims=True)
    o = (_dot_nt(p_past, vt_ref[...]) + _dot(p_new, vn_ref[...])) / l * hm
    out = o[0:tq]
    for h in range(1, N_HEADS):
        out = out + o[h * tq:(h + 1) * tq]
    o_ref[...] = out


def _attn_sample(q, k, qkv, row0, kt_all, vt_all, layer, batch, tq, past):
    q_pos = past + np.arange(tq)
    rep = lambda c: np.concatenate([c] * N_HEADS, axis=0)
    mp = rep(_multiplicity(q_pos, np.arange(past)))
    mn = rep(_multiplicity(q_pos, q_pos))
    row_head = np.arange(N_HEADS * tq) // tq
    hm = (row_head[:, None] == (np.arange(D_ATT) // HEAD_DIM)[None, :]).astype(np.float32)
    rb0 = row0 // tq
    new_spec = pl.BlockSpec((tq, D_ATT), lambda b: (b, 0))
    cache_spec = pl.BlockSpec((None, None, D_ATT, past), lambda b: (layer, b, 0, 0))
    full_spec = lambda a: pl.BlockSpec(a.shape, lambda b: (0,) * a.ndim)
    mp, mn, hm = jnp.asarray(mp), jnp.asarray(mn), jnp.asarray(hm)
    return pl.pallas_call(
        functools.partial(_attn_sample_kernel, tq=tq),
        grid=(batch,),
        in_specs=[new_spec, new_spec, pl.BlockSpec((tq, D_ATT), lambda b: (rb0 + b, 2)),
                  cache_spec, cache_spec, full_spec(mp), full_spec(mn), full_spec(hm)],
        out_specs=new_spec,
        out_shape=jax.ShapeDtypeStruct((batch * tq, D_ATT), F32),
        compiler_params=_cparams(("parallel",)),
        name="attn_sample",
    )(q, k, qkv, kt_all, vt_all, mp, mn, hm)


def _rglru_kernel(x_ref, g_ref, tail0_ref, h0_ref, cw_ref, cb_ref, wa_ref, ba_ref, wx_ref, bx_ref, lam_ref,
                  y_ref, tail_ref, hl_ref, xpad, a_s, b_s, hc, *, batch, tt):
    rows = tt * batch
    pad = (CONV_W - 1) * batch
    i = pl.program_id(0)

    @pl.when(i == 0)
    def _():
        xpad[0:pad, :] = tail0_ref[...]
        hc[...] = h0_ref[...]

    @pl.when(i > 0)
    def _():
        xpad[0:pad, :] = xpad[rows:rows + pad, :]

    xpad[pad:pad + rows, :] = x_ref[...]
    xc = cb_ref[...] + cw_ref[0:1, :] * xpad[0:rows, :]
    for j in range(1, CONV_W):
        xc = xc + cw_ref[j:j + 1, :] * xpad[j * batch:j * batch + rows, :]
    r = jax.nn.sigmoid(_dot(xc, wa_ref[...]) + ba_ref[...])
    ig = jax.nn.sigmoid(_dot(xc, wx_ref[...]) + bx_ref[...])
    log_a = (-LRU_C) * r * _softplus(-lam_ref[...])
    a = jnp.exp(log_a)
    a_s[...] = a
    b_s[...] = jnp.sqrt(-jnp.tanh(log_a) * (a * a + 1.0)) * (ig * xc)

    def step(t, h):
        sl = pl.ds(pl.multiple_of(t * batch, batch), batch)
        h = a_s[sl, :] * h + b_s[sl, :]
        a_s[sl, :] = h
        return h

    h = lax.fori_loop(0, tt, step, hc[...])
    hc[...] = h
    y_ref[...] = a_s[...] * jax.nn.gelu(g_ref[...])
    tail_ref[...] = xpad[rows:rows + pad, :]
    hl_ref[...] = h


def _block_diag(w):
    nb, n, _ = w.shape
    eye = jnp.eye(nb, dtype=w.dtype)
    return (w[:, :, None, :] * eye[:, None, :, None]).reshape(nb * n, nb * n)


def _rglru(rnn_tm, tail0, h0, p, batch, seq, tt):
    rows = tt * batch
    pad = (CONV_W - 1) * batch
    row_spec = pl.BlockSpec((rows, D_ATT), lambda i: (i, 0))
    const = lambda shape: pl.BlockSpec(shape, lambda i: (0,) * len(shape))
    vec = const((1, D_ATT))
    x_tm = g_tm = rnn_tm
    return pl.pallas_call(
        functools.partial(_rglru_kernel, batch=batch, tt=tt),
        grid=(seq // tt,),
        in_specs=[row_spec, pl.BlockSpec((rows, D_ATT), lambda i: (i, 1)), const((pad, D_ATT)),
                  const((batch, D_ATT)), const((CONV_W, D_ATT)), vec,
                  const((D_ATT, D_ATT)), vec, const((D_ATT, D_ATT)), vec, vec],
        out_specs=[row_spec, const((pad, D_ATT)), const((batch, D_ATT))],
        out_shape=[jax.ShapeDtypeStruct((seq * batch, D_ATT), F32),
                   jax.ShapeDtypeStruct((pad, D_ATT), F32),
                   jax.ShapeDtypeStruct((batch, D_ATT), F32)],
        scratch_shapes=[pltpu.VMEM((pad + rows, D_ATT), F32), pltpu.VMEM((rows, D_ATT), F32),
                        pltpu.VMEM((rows, D_ATT), F32), pltpu.VMEM((batch, D_ATT), F32)],
        compiler_params=_cparams(("arbitrary",)),
        name="rglru",
    )(x_tm, g_tm, tail0, h0, p["conv_w"], p["conv_b"].reshape(1, D_ATT),
      _block_diag(p["rg_wa"]).astype(BF16), p["rg_ba"].reshape(1, D_ATT),
      _block_diag(p["rg_wx"]).astype(BF16), p["rg_bx"].reshape(1, D_ATT), p["rg_lambda"].reshape(1, D_ATT))


def _rwkv_token_math(c, prev, mu, w0, w2, a0, a2, g2, kk_w, ka_w, ones):
    cm = c + (prev - c) * mu
    r = cm[:, 0:D_ATT]
    k = cm[:, D_ATT:2 * D_ATT]
    v = cm[:, 2 * D_ATT:3 * D_ATT]
    lo = cm[:, 3 * D_ATT:3 * D_ATT + LORA_W + LORA_A]
    g_lo = cm[:, 3 * D_ATT + LORA_W + LORA_A:]
    w = -_softplus(-(w0 + _dot(jnp.tanh(lo), w2))) - 0.5
    a = jax.nn.sigmoid(a0 + _dot(lo, a2))
    g = _dot(jax.nn.sigmoid(g_lo), g2)
    kk = k * kk_w
    nrm = jnp.sqrt(_dot_split(kk * kk, ones, 2))
    return r, k * (1.0 + (a - 1.0) * ka_w), v, kk / jnp.maximum(nrm, 1e-12), a, -jnp.exp(w), g


def _rwkv_pre_kernel(c_ref, prev_ref, mu_ref, w0_ref, w2_ref, a0_ref, a2_ref, g2_ref, kk_ref, ka_ref, ones_ref,
                     *out_refs):
    vals = _rwkv_token_math(c_ref[...], prev_ref[...], mu_ref[...], w0_ref[...], w2_ref[...], a0_ref[...],
                            a2_ref[...], g2_ref[...], kk_ref[...], ka_ref[...], ones_ref[...])
    for ref, val in zip(out_refs, vals):
        ref[...] = val


def _rwkv_token_params(p):
    w2p = jnp.concatenate([p["rwkv_w2"], jnp.zeros((LORA_A, D_ATT), F32)], axis=0).astype(BF16)
    a2p = jnp.concatenate([jnp.zeros((LORA_W, D_ATT), F32), p["rwkv_a2"]], axis=0).astype(BF16)
    return (p["rwkv_mu"].reshape(1, C_COLS), p["rwkv_w0"].reshape(1, D_ATT), w2p, p["rwkv_a0"].reshape(1, D_ATT),
            a2p, p["rwkv_g2"].astype(BF16), p["rwkv_kk"].reshape(1, D_ATT), p["rwkv_ka"].reshape(1, D_ATT),
            _head_ones())


def _rwkv_pre(cc, prev, p, tm=256):
    n = cc.shape[0]
    params = _rwkv_token_params(p)
    row = pl.BlockSpec((tm, C_COLS), lambda i: (i, 0))
    const = lambda a: pl.BlockSpec(a.shape, lambda i: (0,) * a.ndim)
    out = pl.BlockSpec((tm, D_ATT), lambda i: (i, 0))
    return pl.pallas_call(
        _rwkv_pre_kernel,
        grid=(n // tm,),
        in_specs=[row, row] + [const(a) for a in params],
        out_specs=[out] * 7,
        out_shape=[jax.ShapeDtypeStruct((n, D_ATT), F32)] * 7,
        compiler_params=_cparams(("parallel",)),
        name="rwkv_pre",
    )(cc, prev, *params)


def _wkv_kernel(*refs, c, bb):
    c_refs = refs[:bb]
    (shift0_ref, s0_ref, mu_ref, w0_ref, w2_ref, a0_ref, a2_ref, g2_ref, kk_ref, ka_ref, ones_ref,
     rk_ref, lnw_ref, lnb_ref, tri_ref, sl_ref, il_ref, eye_ref, bm_ref,
     o_ref, sl_out_ref, s_scr, y_scr, last_scr) = refs[bb:]
    ci = pl.program_id(1)

    @pl.when(ci == 0)
    def _():
        s_scr[...] = s0_ref[...]
        last_scr[...] = shift0_ref[...]

    rk, lnw, lnb, tri, strict, incl, eye, bmask, ones = (
        rk_ref[...], lnw_ref[...], lnb_ref[...], tri_ref[...], sl_ref[...], il_ref[...], eye_ref[...],
        bm_ref[...], ones_ref[...])
    token_params = (mu_ref[...], w0_ref[...], w2_ref[...], a0_ref[...], a2_ref[...], g2_ref[...], kk_ref[...],
                    ka_ref[...], ones)
    first_row = lax.broadcasted_iota(jnp.int32, (bb * c, 1), 0) % c == 0
    lane = lax.broadcasted_iota(jnp.int32, (1, LANES), 1)
    first = lane < HEAD_DIM
    stack = lambda z: jnp.concatenate([jnp.where(first, z, 0.0), jnp.where(first, 0.0, z)], axis=0)
    twice = lambda z: jnp.concatenate([z, z], axis=0)
    fold = lambda z: z[:c] + z[c:]
    n_fac = int(np.log2(c))
    two_c = 2 * c

    cols = jnp.concatenate([c_refs[bi][0] for bi in range(bb)], axis=0)
    before = jnp.concatenate([jnp.broadcast_to(last_scr[bi], (c, C_COLS)) for bi in range(bb)], axis=0)
    prev = jnp.where(first_row, before, pltpu.roll(cols, 1, 0))
    for bi in range(bb):
        last_scr[bi] = cols[(bi + 1) * c - 1:(bi + 1) * c, :]
    tokens = _rwkv_token_math(cols, prev, *token_params)
    cum_all = _split_dot_left(tri, tokens[5], 3)
    seqs = []
    for bi in range(bb):
        r, k, v, kap, a, lw, g = (z[bi * c:(bi + 1) * c] for z in tokens)
        cum = cum_all[bi * c:(bi + 1) * c]
        tot = cum[c - 1:c, :]
        e_neg, e_end = jnp.exp(-cum), jnp.exp(tot - cum)
        b = kap * a
        seqs.append(dict(r=r, k=k, v=v, g=g, kap_t=kap * jnp.exp(cum - lw), r_t=r * jnp.exp(cum), k_t=k * e_neg,
                         b_t=b * e_neg, k_h=k * e_end, b_h=b * e_end, g_end=jnp.exp(tot)))
    chains = [(bi, hp, slice(hp * LANES, (hp + 1) * LANES)) for bi in range(bb) for hp in range(HEAD_PAIRS)]

    grams, states, v_stacks = [], [], []
    for bi, hp, sl in chains:
        q = seqs[bi]
        grams.append(_dot_nt(jnp.concatenate([stack(q["kap_t"][:, sl]), stack(q["r_t"][:, sl])], axis=0),
                             jnp.concatenate([twice(q["k_t"][:, sl]), twice(q["b_t"][:, sl])], axis=0)))
        states.append(s_scr[bi, hp])
        v_stacks.append(stack(q["v"][:, sl]))
    from_state = [_dot_nt(jnp.concatenate([seqs[bi]["kap_t"][:, sl], seqs[bi]["r_t"][:, sl]], axis=0), s_p)
                  for (bi, hp, sl), s_p in zip(chains, states)]
    a_kk = [gm[:two_c, :two_c] * strict for gm in grams]
    y_mat = [jnp.concatenate([gm[two_c:, :two_c] * incl, -(gm[two_c:, two_c:] * incl)], axis=1) for gm in grams]
    pw = [-(gm[:two_c, two_c:] * strict) for gm in grams]
    inv = [eye + m for m in pw]
    for _ in range(n_fac - 1):
        pw = [_dot(m, m) for m in pw]
        inv = [t + _dot(t, m) for t, m in zip(inv, pw)]
    x_s = [stack(fs[:c]) + _dot(am, vs) for fs, am, vs in zip(from_state, a_kk, v_stacks)]
    u_s = [_dot(t, x) for t, x in zip(inv, x_s)]
    y_s = [stack(fs[c:]) + _dot(ym, jnp.concatenate([vs, u], axis=0))
           for fs, ym, vs, u in zip(from_state, y_mat, v_stacks, u_s)]
    upd = [_dot(jnp.concatenate([seqs[bi]["v"][:, sl], fold(u)], axis=0).T,
                jnp.concatenate([seqs[bi]["k_h"][:, sl], -seqs[bi]["b_h"][:, sl]], axis=0))
           for (bi, hp, sl), u in zip(chains, u_s)]
    for (bi, hp, sl), s_p, d_s, y in zip(chains, states, upd, y_s):
        s_scr[bi, hp] = s_p * seqs[bi]["g_end"][:, sl] + bmask * d_s
        y_scr[bi * c:(bi + 1) * c, sl] = fold(y)

    r, k, v, g = tokens[0], tokens[1], tokens[2], tokens[6]
    inv_n = 1.0 / HEAD_DIM
    y = y_scr[...]
    mean = _dot_split(y, ones, 3) * inv_n
    d = y - mean
    var = _dot_split(d * d, ones, 3) * inv_n
    yn = d * lax.rsqrt(var + GN_EPS) * lnw + lnb
    out = (yn + _dot_split(r * k * rk, ones, 3) * v) * g
    for bi in range(bb):
        o_ref[bi, 0] = out[bi * c:(bi + 1) * c]

    @pl.when(ci == pl.num_programs(1) - 1)
    def _():
        sl_out_ref[...] = s_scr[...]


def _wkv(cc, shift0, s0_pairs, p, batch, seq, c=64, bb=4):
    assert 2 * c == LANES
    nc = seq // c
    cc3 = cc.reshape(batch * nc, c, C_COLS)
    token_params = _rwkv_token_params(p)
    idx = np.arange(2 * c)
    same = (idx[:, None] // c) == (idx[None, :] // c)
    strict = (same & (idx[None, :] % c < idx[:, None] % c)).astype(np.float32)
    incl = (same & (idx[None, :] % c <= idx[:, None] % c)).astype(np.float32)
    eye = np.eye(2 * c, dtype=np.float32)
    half = np.arange(LANES) // HEAD_DIM
    bmask = (half[:, None] == half[None, :]).astype(np.float32)
    rows = np.arange(bb * c)
    tri = ((rows[None, :] <= rows[:, None]) & (rows[None, :] // c == rows[:, None] // c)).astype(np.float32)

    def chunk(j):
        return pl.BlockSpec((1, c, C_COLS), lambda g, i: ((g * bb + j) * nc + i, 0, 0))

    const = lambda a: pl.BlockSpec(a.shape, lambda g, i: (0,) * a.ndim)
    state = pl.BlockSpec((bb, HEAD_PAIRS, LANES, LANES), lambda g, i: (g, 0, 0, 0))
    consts = token_params + (p["rwkv_rk"].reshape(1, D_ATT), p["rwkv_lnw"].reshape(1, D_ATT),
                             p["rwkv_lnb"].reshape(1, D_ATT), jnp.asarray(tri, dtype=BF16), jnp.asarray(strict),
                             jnp.asarray(incl), jnp.asarray(eye), jnp.asarray(bmask))
    return pl.pallas_call(
        functools.partial(_wkv_kernel, c=c, bb=bb),
        grid=(batch // bb, nc),
        in_specs=[chunk(j) for j in range(bb)]
        + [pl.BlockSpec((bb, 1, C_COLS), lambda g, i: (g, 0, 0)), state] + [const(a) for a in consts],
        out_specs=[pl.BlockSpec((bb, 1, c, D_ATT), lambda g, i: (g, i, 0, 0)), state],
        out_shape=[jax.ShapeDtypeStruct((batch, nc, c, D_ATT), F32),
                   jax.ShapeDtypeStruct((batch, HEAD_PAIRS, LANES, LANES), F32)],
        scratch_shapes=[pltpu.VMEM((bb, HEAD_PAIRS, LANES, LANES), F32), pltpu.VMEM((bb * c, D_ATT), F32),
                        pltpu.VMEM((bb, 1, C_COLS), F32)],
        compiler_params=_cparams(("parallel", "arbitrary")),
        name="wkv",
    )(*([cc3] * bb), shift0.reshape(batch, 1, C_COLS), s0_pairs, *consts)


def _wkv_lane_kernel(r_ref, k_ref, v_ref, kap_ref, a_ref, lw_ref, g_ref, s0_ref, rk_ref, lnw_ref, lnb_ref,
                     o_ref, s_ref, *, steps):
    s_ref[...] = s0_ref[...]
    inv_n = 1.0 / HEAD_DIM
    for t in range(steps):
        kap = kap_ref[t]
        w = jnp.exp(lw_ref[t])
        bt = kap * a_ref[t]
        kt = k_ref[t]
        rt = r_ref[t]

        def per_v(vi, _, t=t, kap=kap, w=w, bt=bt, kt=kt, rt=rt):
            s = s_ref[vi]
            sa = -jnp.sum(s * kap, axis=0, keepdims=True)
            vv = v_ref[t, pl.ds(vi, 1), :]
            s = s * w + sa * bt + vv * kt
            s_ref[vi] = s
            o_ref[t, pl.ds(vi, 1), :] = jnp.sum(s * rt, axis=0, keepdims=True)
            return 0

        lax.fori_loop(0, HEAD_DIM, per_v, 0, unroll=4)
        y = o_ref[t]
        mean = jnp.sum(y, axis=0, keepdims=True) * inv_n
        d = y - mean
        var = jnp.sum(d * d, axis=0, keepdims=True) * inv_n
        yn = d * lax.rsqrt(var + GN_EPS) * lnw_ref[...] + lnb_ref[...]
        bonus = jnp.sum(rt * kt * rk_ref[...], axis=0, keepdims=True) * v_ref[t]
        o_ref[t] = (yn + bonus) * g_ref[t]


def _wkv_lane(parts_t, s0_t, p, batch, steps):
    tok = pl.BlockSpec((steps, HEAD_DIM, batch), lambda h: (0, h, 0))
    st = pl.BlockSpec((HEAD_DIM, HEAD_DIM, batch), lambda h: (h, 0, 0))
    col = pl.BlockSpec((HEAD_DIM, batch), lambda h: (h, 0))
    bcast = lambda z: jnp.broadcast_to(z.reshape(D_ATT, 1), (D_ATT, batch))
    return pl.pallas_call(
        functools.partial(_wkv_lane_kernel, steps=steps),
        grid=(N_HEADS,),
        in_specs=[tok] * 7 + [st, col, col, col],
        out_specs=[tok, st],
        out_shape=[jax.ShapeDtypeStruct((steps, D_ATT, batch), F32),
                   jax.ShapeDtypeStruct((D_ATT, HEAD_DIM, batch), F32)],
        compiler_params=_cparams(("parallel",)),
        name="wkv_lane",
    )(*parts_t, s0_t, bcast(p["rwkv_rk"]), bcast(p["rwkv_lnw"]), bcast(p["rwkv_lnb"]))


def _pair_states(s):
    b = s.shape[0]
    s = s.reshape(b, HEAD_PAIRS, 2, HEAD_DIM, HEAD_DIM)
    z = jnp.zeros((b, HEAD_PAIRS, HEAD_DIM, HEAD_DIM), s.dtype)
    top = jnp.concatenate([s[:, :, 0], z], axis=-1)
    bot = jnp.concatenate([z, s[:, :, 1]], axis=-1)
    return jnp.concatenate([top, bot], axis=-2)


def _unpair_states(sp):
    b = sp.shape[0]
    first = sp[:, :, :HEAD_DIM, :HEAD_DIM]
    second = sp[:, :, HEAD_DIM:, HEAD_DIM:]
    return jnp.stack([first, second], axis=2).reshape(b, N_HEADS, HEAD_DIM, HEAD_DIM)


def _merge_kernel(oa_ref, ob_ref, oc_ref, gl_ref, x_ref, bm_ref, wa_ref, wb_ref, wc_ref, wo_ref, nw_ref,
                  wgh_ref, wgl_ref, bg_ref, x1_ref, xn_ref, comb_ref):
    merged = None
    for j, (o_ref, w_ref) in enumerate(((oa_ref, wa_ref), (ob_ref, wb_ref), (oc_ref, wc_ref))):
        cols = slice(j * D_MODEL, (j + 1) * D_MODEL)
        gate = jax.nn.sigmoid(gl_ref[:, cols] + bm_ref[:, cols])
        t = gate * _dot(o_ref[...], w_ref[...])
        merged = t if merged is None else merged + t
    x1 = x_ref[...] + _dot(merged, wo_ref[...])
    x1_ref[...] = x1
    xn = x1 * lax.rsqrt(jnp.mean(x1 * x1, axis=-1, keepdims=True) + EPS) * nw_ref[...]
    xn_ref[...] = xn.astype(BF16)
    hi = xn.astype(BF16)
    lo = (xn - hi.astype(F32)).astype(BF16)
    wgh, wgl = wgh_ref[...], wgl_ref[...]
    lg = (jnp.dot(hi, wgh, preferred_element_type=F32) + jnp.dot(lo, wgh, preferred_element_type=F32)
          + jnp.dot(hi, wgl, preferred_element_type=F32)) + bg_ref[...]
    col = lambda j: lg[:, j:j + 1]
    best = col(0)
    gi = jnp.zeros(best.shape, jnp.int32)
    for j in range(1, N_GROUPS):
        better = col(j) > best
        best = jnp.where(better, col(j), best)
        gi = jnp.where(better, j, gi)
    den = jnp.zeros(best.shape, F32)
    for j in range(N_GROUPS):
        den = den + jnp.exp(col(j) - best)
    p_group = 1.0 / den
    le = []
    for j in range(EXP_PER_GROUP):
        z = col(N_GROUPS + j)
        for grp in range(1, N_GROUPS):
            z = jnp.where(gi == grp, col(N_GROUPS + grp * EXP_PER_GROUP + j), z)
        le.append(z)
    v1 = le[0]
    i1 = jnp.zeros(best.shape, jnp.int32)
    for j in range(1, EXP_PER_GROUP):
        better = le[j] > v1
        v1 = jnp.where(better, le[j], v1)
        i1 = jnp.where(better, j, i1)
    v2 = jnp.full(best.shape, -jnp.inf, F32)
    i2 = jnp.zeros(best.shape, jnp.int32)
    for j in range(EXP_PER_GROUP):
        cand = jnp.where(i1 == j, -jnp.inf, le[j])
        better = cand > v2
        v2 = jnp.where(better, cand, v2)
        i2 = jnp.where(better, j, i2)
    e2 = jnp.exp(v2 - v1)
    w1 = p_group / (1.0 + e2)
    w2 = p_group * e2 / (1.0 + e2)
    lane = lax.broadcasted_iota(jnp.int32, lg.shape, 1)
    base = gi * EXP_PER_GROUP
    comb_ref[...] = jnp.where(lane == base + i1, w1, 0.0) + jnp.where(lane == base + i2, w2, 0.0)


def _merge(oa, ob, oc, gl, x, p, wb, tm=256, ob_time_major=None):
    n = x.shape[0]
    ob_spec = pl.BlockSpec((tm, D_ATT), lambda i: (i, 0))
    if ob_time_major is not None:
        batch, seq = ob_time_major
        nt = seq // tm
        ob = ob.reshape(seq, batch * D_ATT)
        ob_spec = pl.BlockSpec((tm, D_ATT), lambda i: (i % nt, i // nt))
    wg = jnp.concatenate([p["moe_wg_group"], p["moe_wg_exp"]], axis=1)
    wg = jnp.pad(wg, ((0, 0), (0, LANES - wg.shape[1])))
    wgh = wg.astype(BF16)
    wgl = (wg - wgh.astype(F32)).astype(BF16)
    bg = jnp.pad(jnp.concatenate([p["moe_bg_group"], p["moe_bg_exp"]]), (0, LANES - N_GROUPS - N_EXPERTS))
    row = lambda w: pl.BlockSpec((tm, w), lambda i: (i, 0))
    const = lambda shape: pl.BlockSpec(shape, lambda i: (0,) * len(shape))
    return pl.pallas_call(
        _merge_kernel,
        grid=(n // tm,),
        in_specs=[row(D_ATT), ob_spec, row(D_ATT), row(GL_COLS), row(D_MODEL), const((1, GL_COLS)),
                  const((D_ATT, D_MODEL)), const((D_ATT, D_MODEL)), const((D_ATT, D_MODEL)),
                  const((D_MODEL, D_MODEL)), const((1, D_MODEL)), const((D_MODEL, LANES)),
                  const((D_MODEL, LANES)), const((1, LANES))],
        out_specs=[row(D_MODEL), row(D_MODEL), row(LANES)],
        out_shape=[jax.ShapeDtypeStruct((n, D_MODEL), F32), jax.ShapeDtypeStruct((n, D_MODEL), BF16),
                   jax.ShapeDtypeStruct((n, LANES), F32)],
        compiler_params=_cparams(("parallel",)),
        name="merge",
    )(oa, ob, oc, gl, x, p["b_merge"].reshape(1, GL_COLS), wb["w_branch_a"], wb["w_branch_b"], wb["w_branch_c"],
      wb["w_out"], p["norm2_w"].reshape(1, D_MODEL), wgh, wgl, bg.reshape(1, LANES))


MOE_PAIR = 2


def _moe_kernel(xn_ref, comb_ref, x1_ref, w1_ref, w3_ref, w2_ref, o_ref, acc):
    e = pl.program_id(1)

    @pl.when(e == 0)
    def _():
        acc[...] = x1_ref[...]

    xb = xn_ref[...]
    lane = lax.broadcasted_iota(jnp.int32, comb_ref.shape, 1)
    comb = comb_ref[...]
    pick = lambda j: jnp.sum(jnp.where(lane == MOE_PAIR * e + j, comb, 0.0), axis=1, keepdims=True)
    hid = lax.broadcasted_iota(jnp.int32, (1, MOE_PAIR * D_EXPERT), 1)
    wt = jnp.where(hid < D_EXPERT, pick(0), pick(1))
    h = jax.nn.silu(jnp.dot(xb, w1_ref[0], preferred_element_type=F32)) * jnp.dot(xb, w3_ref[0],
                                                                                  preferred_element_type=F32)
    acc[...] += _dot(h * wt, w2_ref[0])

    @pl.when(e == pl.num_programs(1) - 1)
    def _():
        o_ref[...] = acc[...]


def _moe_weights(w1, w3, w2):
    groups = N_EXPERTS // MOE_PAIR
    side = lambda w: (w.astype(BF16).reshape(groups, MOE_PAIR, D_MODEL, D_EXPERT).transpose(0, 2, 1, 3)
                      .reshape(groups, D_MODEL, MOE_PAIR * D_EXPERT))
    return side(w1), side(w3), w2.astype(BF16).reshape(groups, MOE_PAIR * D_EXPERT, D_MODEL)


def _moe(xn, comb, x1, w1, w3, w2, tm=1024):
    n = xn.shape[0]
    hid = MOE_PAIR * D_EXPERT
    return pl.pallas_call(
        _moe_kernel,
        grid=(n // tm, N_EXPERTS // MOE_PAIR),
        in_specs=[pl.BlockSpec((tm, D_MODEL), lambda i, e: (i, 0)),
                  pl.BlockSpec((tm, LANES), lambda i, e: (i, 0)),
                  pl.BlockSpec((tm, D_MODEL), lambda i, e: (i, 0)),
                  pl.BlockSpec((1, D_MODEL, hid), lambda i, e: (e, 0, 0)),
                  pl.BlockSpec((1, D_MODEL, hid), lambda i, e: (e, 0, 0)),
                  pl.BlockSpec((1, hid, D_MODEL), lambda i, e: (e, 0, 0))],
        out_specs=pl.BlockSpec((tm, D_MODEL), lambda i, e: (i, 0)),
        out_shape=jax.ShapeDtypeStruct((n, D_MODEL), F32),
        scratch_shapes=[pltpu.VMEM((tm, D_MODEL), F32)],
        compiler_params=_cparams(("parallel", "arbitrary")),
        name="moe",
    )(xn, comb, x1, w1, w3, w2)


def _time_major(z, batch, seq):
    return z.reshape(batch, seq, z.shape[-1]).transpose(1, 0, 2).reshape(seq * batch, z.shape[-1])


def _batch_major(z, batch, seq):
    return z.reshape(seq, batch, z.shape[-1]).transpose(1, 0, 2).reshape(batch * seq, z.shape[-1])


def _group_layer(x, g, p, wb, caches, layer, tabs):
    batch, seq = g["batch"], g["seq"]
    rows = batch * seq
    prompt = g["past"] == 0
    tm = 256
    qkv, rnn, cc, gl = _inproj(x, p["norm1_w"], wb["w_in"], time_major_rnn=(batch, seq) if prompt else None)
    heads = lambda z: z.reshape(batch, seq, N_HEADS, HEAD_DIM)
    tail0 = g["conv"].transpose(1, 0, 2).reshape((CONV_W - 1) * batch, D_ATT)
    if prompt:
        q, k, kt, vt = _qkprep(qkv, 0, rows, tabs, g["table_blocks"], p["q_norm_w"], p["k_norm_w"], tm,
                               cache_batch=batch)
        oa = _attn_prompt(q, k, qkv, batch, seq)
        from_t = lambda z: z.reshape(batch, N_HEADS, HEAD_DIM, seq).transpose(0, 3, 1, 2)
        k_new, v_new = from_t(kt), from_t(vt)
        ob, tail, h_last = _rglru(rnn, tail0, g["h"], p, batch, seq, g["tt"])
        o_wkv, s_pairs = _wkv(cc, g["shift"], _pair_states(g["wkv"]), p, batch, seq)
        oc = o_wkv.reshape(rows, D_ATT)
        s_last = _unpair_states(s_pairs)
    else:
        q, k = _qkprep(qkv, 0, rows, tabs, g["table_blocks"], p["q_norm_w"], p["k_norm_w"], tm)
        oa = _attn_sample(q, k, qkv, 0, caches[0], caches[1], layer, batch, seq, g["past"])
        k_new, v_new = heads(k), heads(qkv[:, 2 * D_ATT:])
        y_tm, tail, h_last = _rglru(_time_major(rnn, batch, seq), tail0, g["h"], p, batch, seq, g["tt"])
        ob = _batch_major(y_tm, batch, seq)
        c3 = cc.reshape(batch, seq, C_COLS)
        prev = jnp.concatenate([g["shift"][:, None, :], c3[:, :-1]], axis=1).reshape(rows, C_COLS)
        to_lanes = lambda z: z.reshape(batch, seq, D_ATT).transpose(1, 2, 0)
        o_t, s_t = _wkv_lane([to_lanes(z) for z in _rwkv_pre(cc, prev, p)], g["wkv"], p, batch, seq)
        oc = o_t.transpose(2, 0, 1).reshape(rows, D_ATT)
        s_last = s_t.reshape(N_HEADS, HEAD_DIM, HEAD_DIM, batch).transpose(3, 0, 1, 2)
    conv_tail = tail.reshape(CONV_W - 1, batch, D_ATT).transpose(1, 0, 2)
    shift_last = cc.reshape(batch, seq, C_COLS)[:, -1]
    x1, xn2, comb = _merge(oa, ob, oc, gl, x, p, wb, ob_time_major=(batch, seq) if prompt else None)
    x2 = _moe(xn2, comb, x1, wb["moe_w1"], wb["moe_w3"], wb["moe_w2"], tm=min(1024, rows))
    return x2, (k_new, v_new, conv_tail, h_last, shift_last, s_last)


def kernel(x_prompt, x_sample, cache_k_win, cache_v_win, state_conv, state_h, state_shift, state_wkv, norm1_w, w_in, q_norm_w, k_norm_w, conv_w, conv_b, rg_wa, rg_ba, rg_wx, rg_bx, rg_lambda, rwkv_mu, rwkv_w0, rwkv_w2, rwkv_a0, rwkv_a2, rwkv_g2, rwkv_kk, rwkv_ka, rwkv_rk, rwkv_lnw, rwkv_lnb, b_merge, w_branch_a, w_branch_b, w_branch_c, w_out, norm2_w, moe_wg_group, moe_bg_group, moe_wg_exp, moe_bg_exp, moe_w1, moe_w3, moe_w2):
    weights = dict(norm1_w=norm1_w, w_in=w_in, q_norm_w=q_norm_w, k_norm_w=k_norm_w, conv_w=conv_w, conv_b=conv_b,
                   rg_wa=rg_wa, rg_ba=rg_ba, rg_wx=rg_wx, rg_bx=rg_bx, rg_lambda=rg_lambda, rwkv_mu=rwkv_mu,
                   rwkv_w0=rwkv_w0, rwkv_w2=rwkv_w2, rwkv_a0=rwkv_a0, rwkv_a2=rwkv_a2, rwkv_g2=rwkv_g2,
                   rwkv_kk=rwkv_kk, rwkv_ka=rwkv_ka, rwkv_rk=rwkv_rk, rwkv_lnw=rwkv_lnw, rwkv_lnb=rwkv_lnb,
                   b_merge=b_merge, w_branch_a=w_branch_a, w_branch_b=w_branch_b, w_branch_c=w_branch_c,
                   w_out=w_out, norm2_w=norm2_w, moe_wg_group=moe_wg_group, moe_bg_group=moe_bg_group,
                   moe_wg_exp=moe_wg_exp, moe_bg_exp=moe_bg_exp, moe_w1=moe_w1, moe_w3=moe_w3, moe_w2=moe_w2)
    depth = w_in.shape[0]
    bp, tp = x_prompt.shape[0], x_prompt.shape[1]
    bs, ts = x_sample.shape[0], x_sample.shape[1]
    past = cache_k_win.shape[2]
    x_p = x_prompt.reshape(bp * tp, D_MODEL)
    x_s = x_sample.reshape(bs * ts, D_MODEL)
    tm = 256
    tab_p = _rope_tables(jnp.arange(tp, dtype=jnp.int32))
    tab_s = tuple(jnp.tile(t, (tm // ts, 1)) for t in _rope_tables(past + jnp.arange(ts, dtype=jnp.int32)))
    kt_all = cache_k_win.transpose(0, 1, 3, 4, 2).reshape(depth, bs, D_ATT, past)
    vt_all = cache_v_win.transpose(0, 1, 3, 4, 2).reshape(depth, bs, D_ATT, past)
    wkv_t = state_wkv.transpose(0, 2, 3, 4, 1).reshape(depth, D_ATT, HEAD_DIM, bs)
    outs_p = [[] for _ in range(6)]
    outs_s = [[] for _ in range(6)]
    big = ("w_in", "w_branch_a", "w_branch_b", "w_branch_c", "w_out")
    for l in range(depth):
        p = {k_: v_[l] for k_, v_ in weights.items()}
        wb = {k_: p[k_].astype(BF16) for k_ in big}
        wb["moe_w1"], wb["moe_w3"], wb["moe_w2"] = _moe_weights(p["moe_w1"], p["moe_w3"], p["moe_w2"])
        g_p = dict(batch=bp, seq=tp, past=0, table_blocks=tp // tm, tt=64,
                   conv=jnp.zeros((bp, CONV_W - 1, D_ATT), F32), h=jnp.zeros((bp, D_ATT), F32),
                   shift=jnp.zeros((bp, C_COLS), F32), wkv=jnp.zeros((bp, N_HEADS, HEAD_DIM, HEAD_DIM), F32))
        g_s = dict(batch=bs, seq=ts, past=past, table_blocks=1, tt=ts,
                   conv=state_conv[l], h=state_h[l], shift=state_shift[l], wkv=wkv_t[l])
        x_p, st_p = _group_layer(x_p, g_p, p, wb, None, l, tab_p)
        x_s, st_s = _group_layer(x_s, g_s, p, wb, (kt_all, vt_all), l, tab_s)
        for j in range(6):
            outs_p[j].append(st_p[j])
            outs_s[j].append(st_s[j])
    y_prompt = x_p.reshape(bp, tp, D_MODEL)
    y_sample = x_s.reshape(bs, ts, D_MODEL)
    stack = lambda zs: jnp.stack(zs)
    return (y_prompt, y_sample, *[stack(z) for z in outs_p], *[stack(z) for z in outs_s])
```

```python
import functools

import numpy as np
import jax
import jax.numpy as jnp
from jax import lax
from jax.experimental import pallas as pl
from jax.experimental.pallas import tpu as pltpu

F32 = jnp.float32
BF16 = jnp.bfloat16

D_MODEL = 1024
N_HEADS = 8
HEAD_DIM = 64
D_ATT = N_HEADS * HEAD_DIM
ROT_DIM = HEAD_DIM // 4
ROPE_THETA = 500000.0
PATTERNS = ((128, 1), (512, 4), (2048, 16))
CONV_W = 4
LRU_C = 8.0
LORA_W, LORA_A, LORA_G = 64, 64, 128
C_COLS = 3 * D_ATT + LORA_W + LORA_A + LORA_G
N_BRANCH = 3
GL_COLS = N_BRANCH * D_MODEL
N_GROUPS, EXP_PER_GROUP, N_EXPERTS = 4, 4, 16
D_EXPERT = 512
EPS = 1e-6
GN_EPS = 64e-5

LANES = 128
SUBLANES = 8
VMEM_LIMIT = 56 * 1024 * 1024
HEAD_PAIRS = D_ATT // LANES
NEG_BIG = -1e30


def _cparams(sem):
    return pltpu.CompilerParams(dimension_semantics=sem, vmem_limit_bytes=VMEM_LIMIT)


def _dot(a, b):
    return jnp.dot(a.astype(BF16), b.astype(BF16), preferred_element_type=F32)


def _dot_nt(a, b):
    return lax.dot_general(a.astype(BF16), b.astype(BF16), (((1,), (1,)), ((), ())),
                           preferred_element_type=F32)


def _dot_split(x, w_bf16, passes):
    acc = None
    r = x
    for i in range(passes):
        p = r.astype(BF16)
        t = jnp.dot(p, w_bf16, preferred_element_type=F32)
        acc = t if acc is None else acc + t
        if i + 1 < passes:
            r = r - p.astype(F32)
    return acc


def _split_dot_left(w_exact_bf16, x, passes):
    acc = None
    r = x
    for i in range(passes):
        p = r.astype(BF16)
        t = jnp.dot(w_exact_bf16, p, preferred_element_type=F32)
        acc = t if acc is None else acc + t
        if i + 1 < passes:
            r = r - p.astype(F32)
    return acc


def _softplus(x):
    return jnp.maximum(x, 0.0) + jnp.log1p(jnp.exp(-jnp.abs(x)))


def _head_ones():
    h = np.arange(D_ATT) // HEAD_DIM
    return jnp.asarray((h[:, None] == h[None, :]).astype(np.float32), dtype=BF16)


def _inproj_kernel(x_ref, nw_ref, w_ref, qkv_ref, rg_ref, cc_ref, gl_ref):
    x = x_ref[...]
    xn = x * lax.rsqrt(jnp.mean(x * x, axis=-1, keepdims=True) + EPS) * nw_ref[...]
    xb = xn.astype(BF16)
    off = 0
    for ref in (qkv_ref, rg_ref, cc_ref, gl_ref):
        n = ref.shape[-1]
        ref[...] = jnp.dot(xb, w_ref[:, off:off + n], preferred_element_type=F32)
        off += n


def _inproj(x, norm_w, w_in_bf16, tm=256, time_major_rnn=None):
    n = x.shape[0]
    widths = (3 * D_ATT, 2 * D_ATT, C_COLS, GL_COLS)
    in_cols = sum(widths)
    out_specs = [pl.BlockSpec((tm, w), lambda i: (i, 0)) for w in widths]
    out_shape = [jax.ShapeDtypeStruct((n, w), F32) for w in widths]
    if time_major_rnn is not None:
        batch, seq = time_major_rnn
        nt = seq // tm
        out_specs[1] = pl.BlockSpec((tm, widths[1]), lambda i: (i % nt, i // nt))
        out_shape[1] = jax.ShapeDtypeStruct((seq, batch * widths[1]), F32)
    outs = list(pl.pallas_call(
        _inproj_kernel,
        grid=(n // tm,),
        in_specs=[pl.BlockSpec((tm, D_MODEL), lambda i: (i, 0)),
                  pl.BlockSpec((1, D_MODEL), lambda i: (0, 0)),
                  pl.BlockSpec((D_MODEL, in_cols), lambda i: (0, 0))],
        out_specs=out_specs,
        out_shape=out_shape,
        compiler_params=_cparams(("parallel",)),
        name="inproj",
    )(x, norm_w.reshape(1, D_MODEL), w_in_bf16))
    return outs


def _rope_tables(pos):
    half = ROT_DIM // 2
    inv = ROPE_THETA ** (-jnp.arange(half, dtype=F32) / half)
    ang = pos.astype(F32)[:, None] * inv[None, :]
    cos, sin = jnp.cos(ang), jnp.sin(ang)
    t = pos.shape[0]
    rest = HEAD_DIM - ROT_DIM
    c_head = jnp.concatenate([cos, cos, jnp.ones((t, rest), F32)], axis=1)
    s_prev = jnp.concatenate([jnp.zeros((t, half), F32), sin, jnp.zeros((t, rest), F32)], axis=1)
    s_next = jnp.concatenate([-sin, jnp.zeros((t, half), F32), jnp.zeros((t, rest), F32)], axis=1)
    tile = lambda z: jnp.tile(z, (1, N_HEADS))
    return tile(c_head), tile(s_prev), tile(s_next)


def _qkprep_kernel(q_ref, k_ref, v_ref, cos_ref, sp_ref, sn_ref, qw_ref, kw_ref, ones_ref, qo_ref, ko_ref,
                   *cache_refs):
    half = ROT_DIM // 2
    cos, sp, sn = cos_ref[...], sp_ref[...], sn_ref[...]
    for src, w_ref, dst, scale in ((q_ref, qw_ref, qo_ref, HEAD_DIM ** -0.5), (k_ref, kw_ref, ko_ref, None)):
        x = src[...]
        ms = _dot_split(x * x, ones_ref[...], 2) * (1.0 / HEAD_DIM)
        xn = x * lax.rsqrt(ms + EPS) * w_ref[...]
        y = xn * cos + pltpu.roll(xn, half, 1) * sp + pltpu.roll(xn, D_ATT - half, 1) * sn
        dst[...] = y if scale is None else y * scale
    if cache_refs:
        kt_ref, vt_ref = cache_refs
        kt_ref[...] = ko_ref[...].T
        vt_ref[...] = v_ref[...].T


def _qkprep(qkv, row0, nrows, tables, table_blocks, qw, kw, tm, cache_batch=0):
    cos, sp, sn = tables
    rb0 = row0 // tm
    tab_spec = pl.BlockSpec((tm, D_ATT), lambda i: (i % table_blocks, 0))
    vec = pl.BlockSpec((1, D_ATT), lambda i: (0, 0))
    out_specs = [pl.BlockSpec((tm, D_ATT), lambda i: (i, 0))] * 2
    out_shape = [jax.ShapeDtypeStruct((nrows, D_ATT), F32)] * 2
    if cache_batch:
        out_specs += [pl.BlockSpec((None, D_ATT, tm), lambda i: (i // table_blocks, 0, i % table_blocks))] * 2
        out_shape += [jax.ShapeDtypeStruct((cache_batch, D_ATT, table_blocks * tm), F32)] * 2
    return pl.pallas_call(
        _qkprep_kernel,
        grid=(nrows // tm,),
        in_specs=[pl.BlockSpec((tm, D_ATT), lambda i: (rb0 + i, 0)),
                  pl.BlockSpec((tm, D_ATT), lambda i: (rb0 + i, 1)),
                  pl.BlockSpec((tm, D_ATT), lambda i: (rb0 + i, 2)),
                  tab_spec, tab_spec, tab_spec, vec, vec,
                  pl.BlockSpec((D_ATT, D_ATT), lambda i: (0, 0))],
        out_specs=out_specs,
        out_shape=out_shape,
        compiler_params=_cparams(("parallel",)),
        name="qkprep",
    )(qkv, qkv, qkv, cos, sp, sn, jnp.tile(qw, N_HEADS).reshape(1, D_ATT),
      jnp.tile(kw, N_HEADS).reshape(1, D_ATT), _head_ones())


def _multiplicity(q_pos, k_pos):
    d = q_pos[:, None] - k_pos[None, :]
    c = np.zeros(d.shape, np.float32)
    for window, dil in PATTERNS:
        c += ((d >= 0) & (d <= window) & (d % dil == 0) & (k_pos[None, :] >= 0)).astype(np.float32)
    return c


def _attn_prompt_kernel(q_ref, k_ref, v_ref, mask_ref, o_ref, *, tq):
    qi = pl.program_id(1)
    lane = lax.broadcasted_iota(jnp.int32, (1, LANES), 1)
    first = lane < HEAD_DIM
    qms = []
    for hp in range(HEAD_PAIRS):
        qp = q_ref[:, hp * LANES:(hp + 1) * LANES]
        qms.append(jnp.where(first, qp, 0.0).astype(BF16))
        qms.append(jnp.where(first, 0.0, qp).astype(BF16))

    def body(kj, carry):
        rows = pl.ds(pl.multiple_of(kj * tq, tq), tq)
        c = mask_ref[qi - kj]
        valid = c > 0.0
        new = []
        pair = lambda h: slice((h // 2) * LANES, (h // 2 + 1) * LANES)
        score = lambda h: _dot_nt(qms[h], k_ref[rows, pair(h)])
        s_next = score(0)
        for h in range(N_HEADS):
            sl = pair(h)
            m, l, acc = carry[h]
            s = s_next
            if h + 1 < N_HEADS:
                s_next = score(h + 1)
            sm = jnp.where(valid, s, NEG_BIG)
            m_new = jnp.maximum(m, jnp.max(sm, axis=1, keepdims=True))
            alpha = jnp.exp(m - m_new)
            p = jnp.exp(sm - m_new) * c
            l = alpha * l + jnp.sum(p, axis=1, keepdims=True)
            acc = alpha * acc + _dot(p, v_ref[rows, sl])
            new.append((m_new, l, acc))
        return tuple(new)

    init = tuple((jnp.full((tq, 1), NEG_BIG, F32), jnp.zeros((tq, 1), F32), jnp.zeros((tq, LANES), F32))
                 for _ in range(N_HEADS))
    res = lax.fori_loop(0, qi + 1, body, init)
    for hp in range(HEAD_PAIRS):
        (_, l0, a0), (_, l1, a1) = res[2 * hp], res[2 * hp + 1]
        o_ref[:, hp * LANES:(hp + 1) * LANES] = jnp.where(first, a0 / l0, a1 / l1)


def _attn_prompt(q, k, qkv, batch, seq, tq=512):
    nb = seq // tq
    pos = np.arange(tq)
    masks = np.stack([_multiplicity(pos + d * tq, pos) for d in range(nb)])
    return pl.pallas_call(
        functools.partial(_attn_prompt_kernel, tq=tq),
        grid=(batch, nb),
        in_specs=[pl.BlockSpec((tq, D_ATT), lambda b, i: (b * nb + i, 0)),
                  pl.BlockSpec((seq, D_ATT), lambda b, i: (b, 0)),
                  pl.BlockSpec((seq, D_ATT), lambda b, i: (b, 2)),
                  pl.BlockSpec((nb, tq, tq), lambda b, i: (0, 0, 0))],
        out_specs=pl.BlockSpec((tq, D_ATT), lambda b, i: (b * nb + i, 0)),
        out_shape=jax.ShapeDtypeStruct((batch * seq, D_ATT), F32),
        compiler_params=_cparams(("parallel", "arbitrary")),
        name="attn_prompt",
    )(q, k, qkv, jnp.asarray(masks))


def _attn_sample_kernel(q_ref, kn_ref, vn_ref, kt_ref, vt_ref, mp_ref, mn_ref, hm_ref, o_ref, *, tq):
    q = q_ref[...]
    hm = hm_ref[...]
    qbd = jnp.concatenate([q] * N_HEADS, axis=0) * hm
    s_past = _dot(qbd, kt_ref[...])
    s_new = _dot_nt(qbd, kn_ref[...])
    c_past, c_new = mp_ref[...], mn_ref[...]
    sm_past = jnp.where(c_past > 0.0, s_past, NEG_BIG)
    sm_new = jnp.where(c_new > 0.0, s_new, NEG_BIG)
    m = jnp.maximum(jnp.max(sm_past, axis=1, keepdims=True), jnp.max(sm_new, axis=1, keepdims=True))
    p_past = jnp.exp(sm_past - m) * c_past
    p_new = jnp.exp(sm_new - m) * c_new
    l = jnp.sum(p_past, axis=1, keepdims=True) + jnp.sum(p_new, axis=1, keepdims=True)
    o = (_dot_nt(p_past, vt_ref[...]) + _dot(p_new, vn_ref[...])) / l * hm
    out = o[0:tq]
    for h in range(1, N_HEADS):
        out = out + o[h * tq:(h + 1) * tq]
    o_ref[...] = out


def _attn_sample(q, k, qkv, row0, kt_all, vt_all, layer, batch, tq, past):
    q_pos = past + np.arange(tq)
    rep = lambda c: np.concatenate([c] * N_HEADS, axis=0)
    mp = rep(_multiplicity(q_pos, np.arange(past)))
    mn = rep(_multiplicity(q_pos, q_pos))
    row_head = np.arange(N_HEADS * tq) // tq
    hm = (row_head[:, None] == (np.arange(D_ATT) // HEAD_DIM)[None, :]).astype(np.float32)
    rb0 = row0 // tq
    new_spec = pl.BlockSpec((tq, D_ATT), lambda b: (b, 0))
    cache_spec = pl.BlockSpec((None, None, D_ATT, past), lambda b: (layer, b, 0, 0))
    full_spec = lambda a: pl.BlockSpec(a.shape, lambda b: (0,) * a.ndim)
    mp, mn, hm = jnp.asarray(mp), jnp.asarray(mn), jnp.asarray(hm)
    return pl.pallas_call(
        functools.partial(_attn_sample_kernel, tq=tq),
        grid=(batch,),
        in_specs=[new_spec, new_spec, pl.BlockSpec((tq, D_ATT), lambda b: (rb0 + b, 2)),
                  cache_spec, cache_spec, full_spec(mp), full_spec(mn), full_spec(hm)],
        out_specs=new_spec,
        out_shape=jax.ShapeDtypeStruct((batch * tq, D_ATT), F32),
        compiler_params=_cparams(("parallel",)),
        name="attn_sample",
    )(q, k, qkv, kt_all, vt_all, mp, mn, hm)


def _rglru_kernel(x_ref, g_ref, tail0_ref, h0_ref, cw_ref, cb_ref, wa_ref, ba_ref, wx_ref, bx_ref, lam_ref,
                  y_ref, tail_ref, hl_ref, xpad, a_s, b_s, hc, *, batch, tt):
    rows = tt * batch
    pad = (CONV_W - 1) * batch
    i = pl.program_id(0)

    @pl.when(i == 0)
    def _():
        xpad[0:pad, :] = tail0_ref[...]
        hc[...] = h0_ref[...]

    @pl.when(i > 0)
    def _():
        xpad[0:pad, :] = xpad[rows:rows + pad, :]

    xpad[pad:pad + rows, :] = x_ref[...]
    xc = cb_ref[...] + cw_ref[0:1, :] * xpad[0:rows, :]
    for j in range(1, CONV_W):
        xc = xc + cw_ref[j:j + 1, :] * xpad[j * batch:j * batch + rows, :]
    r = jax.nn.sigmoid(_dot(xc, wa_ref[...]) + ba_ref[...])
    ig = jax.nn.sigmoid(_dot(xc, wx_ref[...]) + bx_ref[...])
    log_a = (-LRU_C) * r * _softplus(-lam_ref[...])
    a = jnp.exp(log_a)
    a_s[...] = a
    b_s[...] = jnp.sqrt(-jnp.tanh(log_a) * (a * a + 1.0)) * (ig * xc)

    def step(t, h):
        sl = pl.ds(pl.multiple_of(t * batch, batch), batch)
        h = a_s[sl, :] * h + b_s[sl, :]
        a_s[sl, :] = h
        return h

    h = lax.fori_loop(0, tt, step, hc[...])
    hc[...] = h
    y_ref[...] = a_s[...] * jax.nn.gelu(g_ref[...])
    tail_ref[...] = xpad[rows:rows + pad, :]
    hl_ref[...] = h


def _rglru_seq_kernel(x_ref, g_ref, tail0_ref, h0_ref, cw_ref, cb_ref, wa_ref, ba_ref, wx_ref, bx_ref, lam_ref,
                      y_ref, tail_ref, hl_ref, xprev, hc, *, tt):
    @pl.when(pl.program_id(1) == 0)
    def _():
        xprev[...] = tail0_ref[...]
        hc[...] = h0_ref[...]

    x = x_ref[...]
    ext = jnp.concatenate([xprev[...], x], axis=0)
    xc = cb_ref[...] + cw_ref[CONV_W - 1:CONV_W, :] * x
    for j in range(1, CONV_W):
        xc = xc + cw_ref[CONV_W - 1 - j:CONV_W - j, :] * pltpu.roll(ext, j, 0)[SUBLANES:]
    xprev[...] = x[tt - SUBLANES:]
    r = jax.nn.sigmoid(_dot(xc, wa_ref[...]) + ba_ref[...])
    ig = jax.nn.sigmoid(_dot(xc, wx_ref[...]) + bx_ref[...])
    log_a = (-LRU_C) * r * _softplus(-lam_ref[...])
    a = jnp.exp(log_a)
    b = jnp.sqrt(-jnp.tanh(log_a) * (a * a + 1.0)) * (ig * xc)
    row = lax.broadcasted_iota(jnp.int32, (tt, 1), 0)
    s = 1
    while s < tt:
        keep = row >= s
        b = a * jnp.where(keep, pltpu.roll(b, s, 0), 0.0) + b
        a = a * jnp.where(keep, pltpu.roll(a, s, 0), 1.0)
        s *= 2
    h = a * hc[...] + b
    hc[...] = h[tt - 1:tt]
    y_ref[...] = h * jax.nn.gelu(g_ref[...])
    tail_ref[...] = x[tt - SUBLANES:]
    hl_ref[...] = h[tt - 1:tt]


def _rglru_seq(rnn2d, conv, h0, p, batch, seq, tt=256):
    pad = SUBLANES - (CONV_W - 1)
    tail0 = jnp.pad(conv, ((0, 0), (pad, 0), (0, 0)))
    const = lambda shape: pl.BlockSpec(shape, lambda b, i: (0,) * len(shape))
    vec = const((1, D_ATT))
    per_seq = lambda rows: pl.BlockSpec((None, rows, D_ATT), lambda b, i: (b, 0, 0))
    y, tail, h_last = pl.pallas_call(
        functools.partial(_rglru_seq_kernel, tt=tt),
        grid=(batch, seq // tt),
        in_specs=[pl.BlockSpec((tt, D_ATT), lambda b, i: (i, 2 * b)),
                  pl.BlockSpec((tt, D_ATT), lambda b, i: (i, 2 * b + 1)),
                  per_seq(SUBLANES), per_seq(1), const((CONV_W, D_ATT)), vec,
                  const((D_ATT, D_ATT)), vec, const((D_ATT, D_ATT)), vec, vec],
        out_specs=[pl.BlockSpec((tt, D_ATT), lambda b, i: (i, b)), per_seq(SUBLANES), per_seq(1)],
        out_shape=[jax.ShapeDtypeStruct((seq, batch * D_ATT), F32),
                   jax.ShapeDtypeStruct((batch, SUBLANES, D_ATT), F32),
                   jax.ShapeDtypeStruct((batch, 1, D_ATT), F32)],
        scratch_shapes=[pltpu.VMEM((SUBLANES, D_ATT), F32), pltpu.VMEM((1, D_ATT), F32)],
        compiler_params=_cparams(("parallel", "arbitrary")),
        name="rglru_seq",
    )(rnn2d, rnn2d, tail0, h0.reshape(batch, 1, D_ATT), p["conv_w"], p["conv_b"].reshape(1, D_ATT),
      _block_diag(p["rg_wa"]).astype(BF16), p["rg_ba"].reshape(1, D_ATT),
      _block_diag(p["rg_wx"]).astype(BF16), p["rg_bx"].reshape(1, D_ATT), p["rg_lambda"].reshape(1, D_ATT))
    return y, tail[:, pad:, :], h_last.reshape(batch, D_ATT)


def _block_diag(w):
    nb, n, _ = w.shape
    eye = jnp.eye(nb, dtype=w.dtype)
    return (w[:, :, None, :] * eye[:, None, :, None]).reshape(nb * n, nb * n)


def _rglru(rnn_tm, tail0, h0, p, batch, seq, tt):
    rows = tt * batch
    pad = (CONV_W - 1) * batch
    row_spec = pl.BlockSpec((rows, D_ATT), lambda i: (i, 0))
    const = lambda shape: pl.BlockSpec(shape, lambda i: (0,) * len(shape))
    vec = const((1, D_ATT))
    x_tm = g_tm = rnn_tm
    return pl.pallas_call(
        functools.partial(_rglru_kernel, batch=batch, tt=tt),
        grid=(seq // tt,),
        in_specs=[row_spec, pl.BlockSpec((rows, D_ATT), lambda i: (i, 1)), const((pad, D_ATT)),
                  const((batch, D_ATT)), const((CONV_W, D_ATT)), vec,
                  const((D_ATT, D_ATT)), vec, const((D_ATT, D_ATT)), vec, vec],
        out_specs=[row_spec, const((pad, D_ATT)), const((batch, D_ATT))],
        out_shape=[jax.ShapeDtypeStruct((seq * batch, D_ATT), F32),
                   jax.ShapeDtypeStruct((pad, D_ATT), F32),
                   jax.ShapeDtypeStruct((batch, D_ATT), F32)],
        scratch_shapes=[pltpu.VMEM((pad + rows, D_ATT), F32), pltpu.VMEM((rows, D_ATT), F32),
                        pltpu.VMEM((rows, D_ATT), F32), pltpu.VMEM((batch, D_ATT), F32)],
        compiler_params=_cparams(("arbitrary",)),
        name="rglru",
    )(x_tm, g_tm, tail0, h0, p["conv_w"], p["conv_b"].reshape(1, D_ATT),
      _block_diag(p["rg_wa"]).astype(BF16), p["rg_ba"].reshape(1, D_ATT),
      _block_diag(p["rg_wx"]).astype(BF16), p["rg_bx"].reshape(1, D_ATT), p["rg_lambda"].reshape(1, D_ATT))


def _rwkv_token_math(c, prev, mu, w0, w2, a0, a2, g2, kk_w, ka_w, ones):
    cm = c + (prev - c) * mu
    r = cm[:, 0:D_ATT]
    k = cm[:, D_ATT:2 * D_ATT]
    v = cm[:, 2 * D_ATT:3 * D_ATT]
    lo = cm[:, 3 * D_ATT:3 * D_ATT + LORA_W + LORA_A]
    g_lo = cm[:, 3 * D_ATT + LORA_W + LORA_A:]
    w = -_softplus(-(w0 + _dot(jnp.tanh(lo), w2))) - 0.5
    a = jax.nn.sigmoid(a0 + _dot(lo, a2))
    g = _dot(jax.nn.sigmoid(g_lo), g2)
    kk = k * kk_w
    nrm = jnp.sqrt(_dot_split(kk * kk, ones, 2))
    return r, k * (1.0 + (a - 1.0) * ka_w), v, kk / jnp.maximum(nrm, 1e-12), a, -jnp.exp(w), g


def _rwkv_pre_kernel(c_ref, prev_ref, mu_ref, w0_ref, w2_ref, a0_ref, a2_ref, g2_ref, kk_ref, ka_ref, ones_ref,
                     *out_refs):
    vals = _rwkv_token_math(c_ref[...], prev_ref[...], mu_ref[...], w0_ref[...], w2_ref[...], a0_ref[...],
                            a2_ref[...], g2_ref[...], kk_ref[...], ka_ref[...], ones_ref[...])
    for ref, val in zip(out_refs, vals):
        ref[...] = val


def _rwkv_token_params(p):
    w2p = jnp.concatenate([p["rwkv_w2"], jnp.zeros((LORA_A, D_ATT), F32)], axis=0).astype(BF16)
    a2p = jnp.concatenate([jnp.zeros((LORA_W, D_ATT), F32), p["rwkv_a2"]], axis=0).astype(BF16)
    return (p["rwkv_mu"].reshape(1, C_COLS), p["rwkv_w0"].reshape(1, D_ATT), w2p, p["rwkv_a0"].reshape(1, D_ATT),
            a2p, p["rwkv_g2"].astype(BF16), p["rwkv_kk"].reshape(1, D_ATT), p["rwkv_ka"].reshape(1, D_ATT),
            _head_ones())


def _rwkv_pre(cc, prev, p, tm=256):
    n = cc.shape[0]
    params = _rwkv_token_params(p)
    row = pl.BlockSpec((tm, C_COLS), lambda i: (i, 0))
    const = lambda a: pl.BlockSpec(a.shape, lambda i: (0,) * a.ndim)
    out = pl.BlockSpec((tm, D_ATT), lambda i: (i, 0))
    return pl.pallas_call(
        _rwkv_pre_kernel,
        grid=(n // tm,),
        in_specs=[row, row] + [const(a) for a in params],
        out_specs=[out] * 7,
        out_shape=[jax.ShapeDtypeStruct((n, D_ATT), F32)] * 7,
        compiler_params=_cparams(("parallel",)),
        name="rwkv_pre",
    )(cc, prev, *params)


def _wkv_kernel(*refs, c, bb):
    c_refs = refs[:bb]
    (shift0_ref, s0_ref, mu_ref, w0_ref, w2_ref, a0_ref, a2_ref, g2_ref, kk_ref, ka_ref, ones_ref,
     rk_ref, lnw_ref, lnb_ref, tri_ref, sl_ref, il_ref, eye_ref, bm_ref,
     o_ref, sl_out_ref, s_scr, y_scr, last_scr) = refs[bb:]
    ci = pl.program_id(1)

    @pl.when(ci == 0)
    def _():
        s_scr[...] = s0_ref[...]
        last_scr[...] = shift0_ref[...]

    rk, lnw, lnb, tri, strict, incl, eye, bmask, ones = (
        rk_ref[...], lnw_ref[...], lnb_ref[...], tri_ref[...], sl_ref[...], il_ref[...], eye_ref[...],
        bm_ref[...], ones_ref[...])
    token_params = (mu_ref[...], w0_ref[...], w2_ref[...], a0_ref[...], a2_ref[...], g2_ref[...], kk_ref[...],
                    ka_ref[...], ones)
    first_row = lax.broadcasted_iota(jnp.int32, (bb * c, 1), 0) % c == 0
    lane = lax.broadcasted_iota(jnp.int32, (1, LANES), 1)
    first = lane < HEAD_DIM
    stack = lambda z: jnp.concatenate([jnp.where(first, z, 0.0), jnp.where(first, 0.0, z)], axis=0)
    twice = lambda z: jnp.concatenate([z, z], axis=0)
    fold = lambda z: z[:c] + z[c:]
    n_fac = int(np.log2(c))
    two_c = 2 * c

    cols = jnp.concatenate([c_refs[bi][0] for bi in range(bb)], axis=0)
    before = jnp.concatenate([jnp.broadcast_to(last_scr[bi], (c, C_COLS)) for bi in range(bb)], axis=0)
    prev = jnp.where(first_row, before, pltpu.roll(cols, 1, 0))
    for bi in range(bb):
        last_scr[bi] = cols[(bi + 1) * c - 1:(bi + 1) * c, :]
    tokens = _rwkv_token_math(cols, prev, *token_params)
    cum_all = _split_dot_left(tri, tokens[5], 3)
    seqs = []
    for bi in range(bb):
        r, k, v, kap, a, lw, g = (z[bi * c:(bi + 1) * c] for z in tokens)
        cum = cum_all[bi * c:(bi + 1) * c]
        tot = cum[c - 1:c, :]
        e_neg, e_end = jnp.exp(-cum), jnp.exp(tot - cum)
        b = kap * a
        seqs.append(dict(r=r, k=k, v=v, g=g, kap_t=kap * jnp.exp(cum - lw), r_t=r * jnp.exp(cum), k_t=k * e_neg,
                         b_t=b * e_neg, k_h=k * e_end, b_h=b * e_end, g_end=jnp.exp(tot)))
    chains = [(bi, hp, slice(hp * LANES, (hp + 1) * LANES)) for bi in range(bb) for hp in range(HEAD_PAIRS)]

    grams, states, v_stacks = [], [], []
    for bi, hp, sl in chains:
        q = seqs[bi]
        grams.append(_dot_nt(jnp.concatenate([stack(q["kap_t"][:, sl]), stack(q["r_t"][:, sl])], axis=0),
                             jnp.concatenate([twice(q["k_t"][:, sl]), twice(q["b_t"][:, sl])], axis=0)))
        states.append(s_scr[bi, hp])
        v_stacks.append(stack(q["v"][:, sl]))
    from_state = [_dot_nt(jnp.concatenate([seqs[bi]["kap_t"][:, sl], seqs[bi]["r_t"][:, sl]], axis=0), s_p)
                  for (bi, hp, sl), s_p in zip(chains, states)]
    a_kk = [gm[:two_c, :two_c] * strict for gm in grams]
    y_mat = [jnp.concatenate([gm[two_c:, :two_c] * incl, -(gm[two_c:, two_c:] * incl)], axis=1) for gm in grams]
    pw = [-(gm[:two_c, two_c:] * strict) for gm in grams]
    inv = [eye + m for m in pw]
    for _ in range(n_fac - 1):
        pw = [_dot(m, m) for m in pw]
        inv = [t + _dot(t, m) for t, m in zip(inv, pw)]
    x_s = [stack(fs[:c]) + _dot(am, vs) for fs, am, vs in zip(from_state, a_kk, v_stacks)]
    u_s = [_dot(t, x) for t, x in zip(inv, x_s)]
    y_s = [stack(fs[c:]) + _dot(ym, jnp.concatenate([vs, u], axis=0))
           for fs, ym, vs, u in zip(from_state, y_mat, v_stacks, u_s)]
    upd = [_dot(jnp.concatenate([seqs[bi]["v"][:, sl], fold(u)], axis=0).T,
                jnp.concatenate([seqs[bi]["k_h"][:, sl], -seqs[bi]["b_h"][:, sl]], axis=0))
           for (bi, hp, sl), u in zip(chains, u_s)]
    for (bi, hp, sl), s_p, d_s, y in zip(chains, states, upd, y_s):
        s_scr[bi, hp] = s_p * seqs[bi]["g_end"][:, sl] + bmask * d_s
        y_scr[bi * c:(bi + 1) * c, sl] = fold(y)

    r, k, v, g = tokens[0], tokens[1], tokens[2], tokens[6]
    inv_n = 1.0 / HEAD_DIM
    y = y_scr[...]
    mean = _dot_split(y, ones, 3) * inv_n
    d = y - mean
    var = _dot_split(d * d, ones, 3) * inv_n
    yn = d * lax.rsqrt(var + GN_EPS) * lnw + lnb
    out = (yn + _dot_split(r * k * rk, ones, 3) * v) * g
    for bi in range(bb):
        o_ref[bi, 0] = out[bi * c:(bi + 1) * c]

    @pl.when(ci == pl.num_programs(1) - 1)
    def _():
        sl_out_ref[...] = s_scr[...]


def _wkv(cc, shift0, s0_pairs, p, batch, seq, c=64, bb=4):
    assert 2 * c == LANES
    nc = seq // c
    cc3 = cc.reshape(batch * nc, c, C_COLS)
    token_params = _rwkv_token_params(p)
    idx = np.arange(2 * c)
    same = (idx[:, None] // c) == (idx[None, :] // c)
    strict = (same & (idx[None, :] % c < idx[:, None] % c)).astype(np.float32)
    incl = (same & (idx[None, :] % c <= idx[:, None] % c)).astype(np.float32)
    eye = np.eye(2 * c, dtype=np.float32)
    half = np.arange(LANES) // HEAD_DIM
    bmask = (half[:, None] == half[None, :]).astype(np.float32)
    rows = np.arange(bb * c)
    tri = ((rows[None, :] <= rows[:, None]) & (rows[None, :] // c == rows[:, None] // c)).astype(np.float32)

    def chunk(j):
        return pl.BlockSpec((1, c, C_COLS), lambda g, i: ((g * bb + j) * nc + i, 0, 0))

    const = lambda a: pl.BlockSpec(a.shape, lambda g, i: (0,) * a.ndim)
    state = pl.BlockSpec((bb, HEAD_PAIRS, LANES, LANES), lambda g, i: (g, 0, 0, 0))
    consts = token_params + (p["rwkv_rk"].reshape(1, D_ATT), p["rwkv_lnw"].reshape(1, D_ATT),
                             p["rwkv_lnb"].reshape(1, D_ATT), jnp.asarray(tri, dtype=BF16), jnp.asarray(strict),
                             jnp.asarray(incl), jnp.asarray(eye), jnp.asarray(bmask))
    return pl.pallas_call(
        functools.partial(_wkv_kernel, c=c, bb=bb),
        grid=(batch // bb, nc),
        in_specs=[chunk(j) for j in range(bb)]
        + [pl.BlockSpec((bb, 1, C_COLS), lambda g, i: (g, 0, 0)), state] + [const(a) for a in consts],
        out_specs=[pl.BlockSpec((bb, 1, c, D_ATT), lambda g, i: (g, i, 0, 0)), state],
        out_shape=[jax.ShapeDtypeStruct((batch, nc, c, D_ATT), F32),
                   jax.ShapeDtypeStruct((batch, HEAD_PAIRS, LANES, LANES), F32)],
        scratch_shapes=[pltpu.VMEM((bb, HEAD_PAIRS, LANES, LANES), F32), pltpu.VMEM((bb * c, D_ATT), F32),
                        pltpu.VMEM((bb, 1, C_COLS), F32)],
        compiler_params=_cparams(("parallel", "arbitrary")),
        name="wkv",
    )(*([cc3] * bb), shift0.reshape(batch, 1, C_COLS), s0_pairs, *consts)


def _wkv_lane_kernel(r_ref, k_ref, v_ref, kap_ref, a_ref, lw_ref, g_ref, s0_ref, rk_ref, lnw_ref, lnb_ref,
                     o_ref, s_ref, *, steps):
    s_ref[...] = s0_ref[...]
    inv_n = 1.0 / HEAD_DIM
    for t in range(steps):
        kap = kap_ref[t]
        w = jnp.exp(lw_ref[t])
        bt = kap * a_ref[t]
        kt = k_ref[t]
        rt = r_ref[t]

        def per_v(vi, _, t=t, kap=kap, w=w, bt=bt, kt=kt, rt=rt):
            s = s_ref[vi]
            sa = -jnp.sum(s * kap, axis=0, keepdims=True)
            vv = v_ref[t, pl.ds(vi, 1), :]
            s = s * w + sa * bt + vv * kt
            s_ref[vi] = s
            o_ref[t, pl.ds(vi, 1), :] = jnp.sum(s * rt, axis=0, keepdims=True)
            return 0

        lax.fori_loop(0, HEAD_DIM, per_v, 0, unroll=4)
        y = o_ref[t]
        mean = jnp.sum(y, axis=0, keepdims=True) * inv_n
        d = y - mean
        var = jnp.sum(d * d, axis=0, keepdims=True) * inv_n
        yn = d * lax.rsqrt(var + GN_EPS) * lnw_ref[...] + lnb_ref[...]
        bonus = jnp.sum(rt * kt * rk_ref[...], axis=0, keepdims=True) * v_ref[t]
        o_ref[t] = (yn + bonus) * g_ref[t]


def _wkv_lane(parts_t, s0_t, p, batch, steps):
    tok = pl.BlockSpec((steps, HEAD_DIM, batch), lambda h: (0, h, 0))
    st = pl.BlockSpec((HEAD_DIM, HEAD_DIM, batch), lambda h: (h, 0, 0))
    col = pl.BlockSpec((HEAD_DIM, batch), lambda h: (h, 0))
    bcast = lambda z: jnp.broadcast_to(z.reshape(D_ATT, 1), (D_ATT, batch))
    return pl.pallas_call(
        functools.partial(_wkv_lane_kernel, steps=steps),
        grid=(N_HEADS,),
        in_specs=[tok] * 7 + [st, col, col, col],
        out_specs=[tok, st],
        out_shape=[jax.ShapeDtypeStruct((steps, D_ATT, batch), F32),
                   jax.ShapeDtypeStruct((D_ATT, HEAD_DIM, batch), F32)],
        compiler_params=_cparams(("parallel",)),
        name="wkv_lane",
    )(*parts_t, s0_t, bcast(p["rwkv_rk"]), bcast(p["rwkv_lnw"]), bcast(p["rwkv_lnb"]))


def _pair_states(s):
    b = s.shape[0]
    s = s.reshape(b, HEAD_PAIRS, 2, HEAD_DIM, HEAD_DIM)
    z = jnp.zeros((b, HEAD_PAIRS, HEAD_DIM, HEAD_DIM), s.dtype)
    top = jnp.concatenate([s[:, :, 0], z], axis=-1)
    bot = jnp.concatenate([z, s[:, :, 1]], axis=-1)
    return jnp.concatenate([top, bot], axis=-2)


def _unpair_states(sp):
    b = sp.shape[0]
    first = sp[:, :, :HEAD_DIM, :HEAD_DIM]
    second = sp[:, :, HEAD_DIM:, HEAD_DIM:]
    return jnp.stack([first, second], axis=2).reshape(b, N_HEADS, HEAD_DIM, HEAD_DIM)


def _merge_kernel(oa_ref, ob_ref, oc_ref, gl_ref, x_ref, bm_ref, wa_ref, wb_ref, wc_ref, wo_ref, nw_ref,
                  wgh_ref, wgl_ref, bg_ref, x1_ref, xn_ref, comb_ref):
    merged = None
    for j, (o_ref, w_ref) in enumerate(((oa_ref, wa_ref), (ob_ref, wb_ref), (oc_ref, wc_ref))):
        cols = slice(j * D_MODEL, (j + 1) * D_MODEL)
        gate = jax.nn.sigmoid(gl_ref[:, cols] + bm_ref[:, cols])
        t = gate * _dot(o_ref[...], w_ref[...])
        merged = t if merged is None else merged + t
    x1 = x_ref[...] + _dot(merged, wo_ref[...])
    x1_ref[...] = x1
    xn = x1 * lax.rsqrt(jnp.mean(x1 * x1, axis=-1, keepdims=True) + EPS) * nw_ref[...]
    xn_ref[...] = xn.astype(BF16)
    hi = xn.astype(BF16)
    lo = (xn - hi.astype(F32)).astype(BF16)
    wgh, wgl = wgh_ref[...], wgl_ref[...]
    lg = (jnp.dot(hi, wgh, preferred_element_type=F32) + jnp.dot(lo, wgh, preferred_element_type=F32)
          + jnp.dot(hi, wgl, preferred_element_type=F32)) + bg_ref[...]
    col = lambda j: lg[:, j:j + 1]
    best = col(0)
    gi = jnp.zeros(best.shape, jnp.int32)
    for j in range(1, N_GROUPS):
        better = col(j) > best
        best = jnp.where(better, col(j), best)
        gi = jnp.where(better, j, gi)
    den = jnp.zeros(best.shape, F32)
    for j in range(N_GROUPS):
        den = den + jnp.exp(col(j) - best)
    p_group = 1.0 / den
    le = []
    for j in range(EXP_PER_GROUP):
        z = col(N_GROUPS + j)
        for grp in range(1, N_GROUPS):
            z = jnp.where(gi == grp, col(N_GROUPS + grp * EXP_PER_GROUP + j), z)
        le.append(z)
    v1 = le[0]
    i1 = jnp.zeros(best.shape, jnp.int32)
    for j in range(1, EXP_PER_GROUP):
        better = le[j] > v1
        v1 = jnp.where(better, le[j], v1)
        i1 = jnp.where(better, j, i1)
    v2 = jnp.full(best.shape, -jnp.inf, F32)
    i2 = jnp.zeros(best.shape, jnp.int32)
    for j in range(EXP_PER_GROUP):
        cand = jnp.where(i1 == j, -jnp.inf, le[j])
        better = cand > v2
        v2 = jnp.where(better, cand, v2)
        i2 = jnp.where(better, j, i2)
    e2 = jnp.exp(v2 - v1)
    w1 = p_group / (1.0 + e2)
    w2 = p_group * e2 / (1.0 + e2)
    lane = lax.broadcasted_iota(jnp.int32, lg.shape, 1)
    base = gi * EXP_PER_GROUP
    comb_ref[...] = jnp.where(lane == base + i1, w1, 0.0) + jnp.where(lane == base + i2, w2, 0.0)


def _merge(oa, ob, oc, gl, x, p, wb, tm=256, ob_time_major=None):
    n = x.shape[0]
    ob_spec = pl.BlockSpec((tm, D_ATT), lambda i: (i, 0))
    if ob_time_major is not None:
        batch, seq = ob_time_major
        nt = seq // tm
        ob = ob.reshape(seq, batch * D_ATT)
        ob_spec = pl.BlockSpec((tm, D_ATT), lambda i: (i % nt, i // nt))
    wg = jnp.concatenate([p["moe_wg_group"], p["moe_wg_exp"]], axis=1)
    wg = jnp.pad(wg, ((0, 0), (0, LANES - wg.shape[1])))
    wgh = wg.astype(BF16)
    wgl = (wg - wgh.astype(F32)).astype(BF16)
    bg = jnp.pad(jnp.concatenate([p["moe_bg_group"], p["moe_bg_exp"]]), (0, LANES - N_GROUPS - N_EXPERTS))
    row = lambda w: pl.BlockSpec((tm, w), lambda i: (i, 0))
    const = lambda shape: pl.BlockSpec(shape, lambda i: (0,) * len(shape))
    return pl.pallas_call(
        _merge_kernel,
        grid=(n // tm,),
        in_specs=[row(D_ATT), ob_spec, row(D_ATT), row(GL_COLS), row(D_MODEL), const((1, GL_COLS)),
                  const((D_ATT, D_MODEL)), const((D_ATT, D_MODEL)), const((D_ATT, D_MODEL)),
                  const((D_MODEL, D_MODEL)), const((1, D_MODEL)), const((D_MODEL, LANES)),
                  const((D_MODEL, LANES)), const((1, LANES))],
        out_specs=[row(D_MODEL), row(D_MODEL), row(LANES)],
        out_shape=[jax.ShapeDtypeStruct((n, D_MODEL), F32), jax.ShapeDtypeStruct((n, D_MODEL), BF16),
                   jax.ShapeDtypeStruct((n, LANES), F32)],
        compiler_params=_cparams(("parallel",)),
        name="merge",
    )(oa, ob, oc, gl, x, p["b_merge"].reshape(1, GL_COLS), wb["w_branch_a"], wb["w_branch_b"], wb["w_branch_c"],
      wb["w_out"], p["norm2_w"].reshape(1, D_MODEL), wgh, wgl, bg.reshape(1, LANES))


MOE_PAIR = 2


def _moe_kernel(xn_ref, comb_ref, x1_ref, w1_ref, w3_ref, w2_ref, o_ref, acc):
    e = pl.program_id(1)

    @pl.when(e == 0)
    def _():
        acc[...] = x1_ref[...]

    xb = xn_ref[...]
    lane = lax.broadcasted_iota(jnp.int32, comb_ref.shape, 1)
    comb = comb_ref[...]
    pick = lambda j: jnp.sum(jnp.where(lane == MOE_PAIR * e + j, comb, 0.0), axis=1, keepdims=True)
    hidden = []
    for j in range(MOE_PAIR):
        h = jax.nn.silu(jnp.dot(xb, w1_ref[j], preferred_element_type=F32)) * jnp.dot(xb, w3_ref[j],
                                                                                      preferred_element_type=F32)
        hidden.append((h * pick(j)).astype(BF16))
    acc[...] += jnp.dot(jnp.concatenate(hidden, axis=1), w2_ref[0], preferred_element_type=F32)

    @pl.when(e == pl.num_programs(1) - 1)
    def _():
        o_ref[...] = acc[...]


def _moe_weights(w1, w3, w2):
    return (w1.astype(BF16), w3.astype(BF16),
            w2.astype(BF16).reshape(N_EXPERTS // MOE_PAIR, MOE_PAIR * D_EXPERT, D_MODEL))


def _moe(xn, comb, x1, w1, w3, w2, tm=1024):
    n = xn.shape[0]
    hid = MOE_PAIR * D_EXPERT
    return pl.pallas_call(
        _moe_kernel,
        grid=(n // tm, N_EXPERTS // MOE_PAIR),
        in_specs=[pl.BlockSpec((tm, D_MODEL), lambda i, e: (i, 0)),
                  pl.BlockSpec((tm, LANES), lambda i, e: (i, 0)),
                  pl.BlockSpec((tm, D_MODEL), lambda i, e: (i, 0)),
                  pl.BlockSpec((MOE_PAIR, D_MODEL, D_EXPERT), lambda i, e: (e, 0, 0)),
                  pl.BlockSpec((MOE_PAIR, D_MODEL, D_EXPERT), lambda i, e: (e, 0, 0)),
                  pl.BlockSpec((1, hid, D_MODEL), lambda i, e: (e, 0, 0))],
        out_specs=pl.BlockSpec((tm, D_MODEL), lambda i, e: (i, 0)),
        out_shape=jax.ShapeDtypeStruct((n, D_MODEL), F32),
        scratch_shapes=[pltpu.VMEM((tm, D_MODEL), F32)],
        compiler_params=_cparams(("parallel", "arbitrary")),
        name="moe",
    )(xn, comb, x1, w1, w3, w2)


def _time_major(z, batch, seq):
    return z.reshape(batch, seq, z.shape[-1]).transpose(1, 0, 2).reshape(seq * batch, z.shape[-1])


def _batch_major(z, batch, seq):
    return z.reshape(seq, batch, z.shape[-1]).transpose(1, 0, 2).reshape(batch * seq, z.shape[-1])


def _group_layer(x, g, p, wb, caches, layer, tabs):
    batch, seq = g["batch"], g["seq"]
    rows = batch * seq
    prompt = g["past"] == 0
    tm = 256
    qkv, rnn, cc, gl = _inproj(x, p["norm1_w"], wb["w_in"], time_major_rnn=(batch, seq) if prompt else None)
    heads = lambda z: z.reshape(batch, seq, N_HEADS, HEAD_DIM)
    if prompt:
        q, k, kt, vt = _qkprep(qkv, 0, rows, tabs, g["table_blocks"], p["q_norm_w"], p["k_norm_w"], tm,
                               cache_batch=batch)
        oa = _attn_prompt(q, k, qkv, batch, seq)
        from_t = lambda z: z.reshape(batch, N_HEADS, HEAD_DIM, seq).transpose(0, 3, 1, 2)
        k_new, v_new = from_t(kt), from_t(vt)
        ob, conv_tail, h_last = _rglru_seq(rnn, g["conv"], g["h"], p, batch, seq)
        o_wkv, s_pairs = _wkv(cc, g["shift"], _pair_states(g["wkv"]), p, batch, seq)
        oc = o_wkv.reshape(rows, D_ATT)
        s_last = _unpair_states(s_pairs)
    else:
        q, k = _qkprep(qkv, 0, rows, tabs, g["table_blocks"], p["q_norm_w"], p["k_norm_w"], tm)
        oa = _attn_sample(q, k, qkv, 0, caches[0], caches[1], layer, batch, seq, g["past"])
        k_new, v_new = heads(k), heads(qkv[:, 2 * D_ATT:])
        tail0 = g["conv"].transpose(1, 0, 2).reshape((CONV_W - 1) * batch, D_ATT)
        y_tm, tail, h_last = _rglru(_time_major(rnn, batch, seq), tail0, g["h"], p, batch, seq, g["tt"])
        ob = _batch_major(y_tm, batch, seq)
        conv_tail = tail.reshape(CONV_W - 1, batch, D_ATT).transpose(1, 0, 2)
        c3 = cc.reshape(batch, seq, C_COLS)
        prev = jnp.concatenate([g["shift"][:, None, :], c3[:, :-1]], axis=1).reshape(rows, C_COLS)
        to_lanes = lambda z: z.reshape(batch, seq, D_ATT).transpose(1, 2, 0)
        o_t, s_t = _wkv_lane([to_lanes(z) for z in _rwkv_pre(cc, prev, p)], g["wkv"], p, batch, seq)
        oc = o_t.transpose(2, 0, 1).reshape(rows, D_ATT)
        s_last = s_t.reshape(N_HEADS, HEAD_DIM, HEAD_DIM, batch).transpose(3, 0, 1, 2)
    shift_last = cc.reshape(batch, seq, C_COLS)[:, -1]
    x1, xn2, comb = _merge(oa, ob, oc, gl, x, p, wb, ob_time_major=(batch, seq) if prompt else None)
    x2 = _moe(xn2, comb, x1, wb["moe_w1"], wb["moe_w3"], wb["moe_w2"], tm=min(1024, rows))
    return x2, (k_new, v_new, conv_tail, h_last, shift_last, s_last)


def kernel(x_prompt, x_sample, cache_k_win, cache_v_win, state_conv, state_h, state_shift, state_wkv, norm1_w, w_in, q_norm_w, k_norm_w, conv_w, conv_b, rg_wa, rg_ba, rg_wx, rg_bx, rg_lambda, rwkv_mu, rwkv_w0, rwkv_w2, rwkv_a0, rwkv_a2, rwkv_g2, rwkv_kk, rwkv_ka, rwkv_rk, rwkv_lnw, rwkv_lnb, b_merge, w_branch_a, w_branch_b, w_branch_c, w_out, norm2_w, moe_wg_group, moe_bg_group, moe_wg_exp, moe_bg_exp, moe_w1, moe_w3, moe_w2):
    weights = dict(norm1_w=norm1_w, w_in=w_in, q_norm_w=q_norm_w, k_norm_w=k_norm_w, conv_w=conv_w, conv_b=conv_b,
                   rg_wa=rg_wa, rg_ba=rg_ba, rg_wx=rg_wx, rg_bx=rg_bx, rg_lambda=rg_lambda, rwkv_mu=rwkv_mu,
                   rwkv_w0=rwkv_w0, rwkv_w2=rwkv_w2, rwkv_a0=rwkv_a0, rwkv_a2=rwkv_a2, rwkv_g2=rwkv_g2,
                   rwkv_kk=rwkv_kk, rwkv_ka=rwkv_ka, rwkv_rk=rwkv_rk, rwkv_lnw=rwkv_lnw, rwkv_lnb=rwkv_lnb,
                   b_merge=b_merge, w_branch_a=w_branch_a, w_branch_b=w_branch_b, w_branch_c=w_branch_c,
                   w_out=w_out, norm2_w=norm2_w, moe_wg_group=moe_wg_group, moe_bg_group=moe_bg_group,
                   moe_wg_exp=moe_wg_exp, moe_bg_exp=moe_bg_exp, moe_w1=moe_w1, moe_w3=moe_w3, moe_w2=moe_w2)
    depth = w_in.shape[0]
    bp, tp = x_prompt.shape[0], x_prompt.shape[1]
    bs, ts = x_sample.shape[0], x_sample.shape[1]
    past = cache_k_win.shape[2]
    x_p = x_prompt.reshape(bp * tp, D_MODEL)
    x_s = x_sample.reshape(bs * ts, D_MODEL)
    tm = 256
    tab_p = _rope_tables(jnp.arange(tp, dtype=jnp.int32))
    tab_s = tuple(jnp.tile(t, (tm // ts, 1)) for t in _rope_tables(past + jnp.arange(ts, dtype=jnp.int32)))
    kt_all = cache_k_win.transpose(0, 1, 3, 4, 2).reshape(depth, bs, D_ATT, past)
    vt_all = cache_v_win.transpose(0, 1, 3, 4, 2).reshape(depth, bs, D_ATT, past)
    wkv_t = state_wkv.transpose(0, 2, 3, 4, 1).reshape(depth, D_ATT, HEAD_DIM, bs)
    outs_p = [[] for _ in range(6)]
    outs_s = [[] for _ in range(6)]
    big = ("w_in", "w_branch_a", "w_branch_b", "w_branch_c", "w_out")
    for l in range(depth):
        p = {k_: v_[l] for k_, v_ in weights.items()}
        wb = {k_: p[k_].astype(BF16) for k_ in big}
        wb["moe_w1"], wb["moe_w3"], wb["moe_w2"] = _moe_weights(p["moe_w1"], p["moe_w3"], p["moe_w2"])
        g_p = dict(batch=bp, seq=tp, past=0, table_blocks=tp // tm, tt=64,
                   conv=jnp.zeros((bp, CONV_W - 1, D_ATT), F32), h=jnp.zeros((bp, D_ATT), F32),
                   shift=jnp.zeros((bp, C_COLS), F32), wkv=jnp.zeros((bp, N_HEADS, HEAD_DIM, HEAD_DIM), F32))
        g_s = dict(batch=bs, seq=ts, past=past, table_blocks=1, tt=ts,
                   conv=state_conv[l], h=state_h[l], shift=state_shift[l], wkv=wkv_t[l])
        x_p, st_p = _group_layer(x_p, g_p, p, wb, None, l, tab_p)
        x_s, st_s = _group_layer(x_s, g_s, p, wb, (kt_all, vt_all), l, tab_s)
        for j in range(6):
            outs_p[j].append(st_p[j])
            outs_s[j].append(st_s[j])
    y_prompt = x_p.reshape(bp, tp, D_MODEL)
    y_sample = x_s.reshape(bs, ts, D_MODEL)
    stack = lambda zs: jnp.stack(zs)
    return (y_prompt, y_sample, *[stack(z) for z in outs_p], *[stack(z) for z in outs_s])
```

```python
import functools

import numpy as np
import jax
import jax.numpy as jnp
from jax import lax
from jax.experimental import pallas as pl
from jax.experimental.pallas import tpu as pltpu

F32 = jnp.float32
BF16 = jnp.bfloat16

D_MODEL = 1024
N_HEADS = 8
HEAD_DIM = 64
D_ATT = N_HEADS * HEAD_DIM
ROT_DIM = HEAD_DIM // 4
ROPE_THETA = 500000.0
PATTERNS = ((128, 1), (512, 4), (2048, 16))
CONV_W = 4
LRU_C = 8.0
LORA_W, LORA_A, LORA_G = 64, 64, 128
C_COLS = 3 * D_ATT + LORA_W + LORA_A + LORA_G
N_BRANCH = 3
GL_COLS = N_BRANCH * D_MODEL
N_GROUPS, EXP_PER_GROUP, N_EXPERTS = 4, 4, 16
D_EXPERT = 512
EPS = 1e-6
GN_EPS = 64e-5

LANES = 128
SUBLANES = 8
VMEM_LIMIT = 56 * 1024 * 1024
HEAD_PAIRS = D_ATT // LANES
NEG_BIG = -1e30


def _cparams(sem):
    return pltpu.CompilerParams(dimension_semantics=sem, vmem_limit_bytes=VMEM_LIMIT)


def _dot(a, b):
    return jnp.dot(a.astype(BF16), b.astype(BF16), preferred_element_type=F32)


def _dot_nt(a, b):
    return lax.dot_general(a.astype(BF16), b.astype(BF16), (((1,), (1,)), ((), ())),
                           preferred_element_type=F32)


def _dot_split(x, w_bf16, passes):
    acc = None
    r = x
    for i in range(passes):
        p = r.astype(BF16)
        t = jnp.dot(p, w_bf16, preferred_element_type=F32)
        acc = t if acc is None else acc + t
        if i + 1 < passes:
            r = r - p.astype(F32)
    return acc


def _split_dot_left(w_exact_bf16, x, passes):
    acc = None
    r = x
    for i in range(passes):
        p = r.astype(BF16)
        t = jnp.dot(w_exact_bf16, p, preferred_element_type=F32)
        acc = t if acc is None else acc + t
        if i + 1 < passes:
            r = r - p.astype(F32)
    return acc


def _softplus(x):
    return jnp.maximum(x, 0.0) + jnp.log1p(jnp.exp(-jnp.abs(x)))


def _head_ones():
    h = np.arange(D_ATT) // HEAD_DIM
    return jnp.asarray((h[:, None] == h[None, :]).astype(np.float32), dtype=BF16)


def _inproj_kernel(x_ref, nw_ref, w_ref, qkv_ref, rg_ref, cc_ref, gl_ref):
    x = x_ref[...]
    xn = x * lax.rsqrt(jnp.mean(x * x, axis=-1, keepdims=True) + EPS) * nw_ref[...]
    xb = xn.astype(BF16)
    off = 0
    for ref in (qkv_ref, rg_ref, cc_ref, gl_ref):
        n = ref.shape[-1]
        ref[...] = jnp.dot(xb, w_ref[:, off:off + n], preferred_element_type=F32)
        off += n


def _inproj(x, norm_w, w_in_bf16, tm=256, time_major_rnn=None):
    n = x.shape[0]
    widths = (3 * D_ATT, 2 * D_ATT, C_COLS, GL_COLS)
    in_cols = sum(widths)
    out_specs = [pl.BlockSpec((tm, w), lambda i: (i, 0)) for w in widths]
    out_shape = [jax.ShapeDtypeStruct((n, w), F32) for w in widths]
    if time_major_rnn is not None:
        batch, seq = time_major_rnn
        nt = seq // tm
        out_specs[1] = pl.BlockSpec((tm, widths[1]), lambda i: (i % nt, i // nt))
        out_shape[1] = jax.ShapeDtypeStruct((seq, batch * widths[1]), F32)
    outs = list(pl.pallas_call(
        _inproj_kernel,
        grid=(n // tm,),
        in_specs=[pl.BlockSpec((tm, D_MODEL), lambda i: (i, 0)),
                  pl.BlockSpec((1, D_MODEL), lambda i: (0, 0)),
                  pl.BlockSpec((D_MODEL, in_cols), lambda i: (0, 0))],
        out_specs=out_specs,
        out_shape=out_shape,
        compiler_params=_cparams(("parallel",)),
        name="inproj",
    )(x, norm_w.reshape(1, D_MODEL), w_in_bf16))
    return outs


def _rope_tables(pos):
    half = ROT_DIM // 2
    inv = ROPE_THETA ** (-jnp.arange(half, dtype=F32) / half)
    ang = pos.astype(F32)[:, None] * inv[None, :]
    cos, sin = jnp.cos(ang), jnp.sin(ang)
    t = pos.shape[0]
    rest = HEAD_DIM - ROT_DIM
    c_head = jnp.concatenate([cos, cos, jnp.ones((t, rest), F32)], axis=1)
    s_prev = jnp.concatenate([jnp.zeros((t, half), F32), sin, jnp.zeros((t, rest), F32)], axis=1)
    s_next = jnp.concatenate([-sin, jnp.zeros((t, half), F32), jnp.zeros((t, rest), F32)], axis=1)
    tile = lambda z: jnp.tile(z, (1, N_HEADS))
    return tile(c_head), tile(s_prev), tile(s_next)


def _qkprep_kernel(q_ref, k_ref, v_ref, cos_ref, sp_ref, sn_ref, qw_ref, kw_ref, ones_ref, qo_ref, ko_ref,
                   *cache_refs):
    half = ROT_DIM // 2
    cos, sp, sn = cos_ref[...], sp_ref[...], sn_ref[...]
    for src, w_ref, dst, scale in ((q_ref, qw_ref, qo_ref, HEAD_DIM ** -0.5), (k_ref, kw_ref, ko_ref, None)):
        x = src[...]
        ms = _dot_split(x * x, ones_ref[...], 2) * (1.0 / HEAD_DIM)
        xn = x * lax.rsqrt(ms + EPS) * w_ref[...]
        y = xn * cos + pltpu.roll(xn, half, 1) * sp + pltpu.roll(xn, D_ATT - half, 1) * sn
        dst[...] = y if scale is None else y * scale
    if cache_refs:
        kt_ref, vt_ref = cache_refs
        kt_ref[...] = ko_ref[...].T
        vt_ref[...] = v_ref[...].T


def _qkprep(qkv, row0, nrows, tables, table_blocks, qw, kw, tm, cache_batch=0):
    cos, sp, sn = tables
    rb0 = row0 // tm
    n_seq = nrows // (tm * table_blocks)
    row = lambda t, b: b * table_blocks + t
    tab_spec = pl.BlockSpec((tm, D_ATT), lambda t, b: (t, 0))
    vec = pl.BlockSpec((1, D_ATT), lambda t, b: (0, 0))
    out_specs = [pl.BlockSpec((tm, D_ATT), lambda t, b: (row(t, b), 0))] * 2
    out_shape = [jax.ShapeDtypeStruct((nrows, D_ATT), F32)] * 2
    if cache_batch:
        out_specs += [pl.BlockSpec((None, D_ATT, tm), lambda t, b: (b, 0, t))] * 2
        out_shape += [jax.ShapeDtypeStruct((cache_batch, D_ATT, table_blocks * tm), F32)] * 2
    return pl.pallas_call(
        _qkprep_kernel,
        grid=(table_blocks, n_seq),
        in_specs=[pl.BlockSpec((tm, D_ATT), lambda t, b: (rb0 + row(t, b), 0)),
                  pl.BlockSpec((tm, D_ATT), lambda t, b: (rb0 + row(t, b), 1)),
                  pl.BlockSpec((tm, D_ATT), lambda t, b: (rb0 + row(t, b), 2)),
                  tab_spec, tab_spec, tab_spec, vec, vec,
                  pl.BlockSpec((D_ATT, D_ATT), lambda t, b: (0, 0))],
        out_specs=out_specs,
        out_shape=out_shape,
        compiler_params=_cparams(("parallel", "parallel")),
        name="qkprep",
    )(qkv, qkv, qkv, cos, sp, sn, jnp.tile(qw, N_HEADS).reshape(1, D_ATT),
      jnp.tile(kw, N_HEADS).reshape(1, D_ATT), _head_ones())


def _multiplicity(q_pos, k_pos):
    d = q_pos[:, None] - k_pos[None, :]
    c = np.zeros(d.shape, np.float32)
    for window, dil in PATTERNS:
        c += ((d >= 0) & (d <= window) & (d % dil == 0) & (k_pos[None, :] >= 0)).astype(np.float32)
    return c


def _attn_prompt_kernel(q_ref, k_ref, v_ref, mask_ref, o_ref, *, tq):
    qi = pl.program_id(1)
    lane = lax.broadcasted_iota(jnp.int32, (1, LANES), 1)
    first = lane < HEAD_DIM
    qms = []
    for hp in range(HEAD_PAIRS):
        qp = q_ref[:, hp * LANES:(hp + 1) * LANES]
        qms.append(jnp.where(first, qp, 0.0).astype(BF16))
        qms.append(jnp.where(first, 0.0, qp).astype(BF16))

    def body(kj, carry):
        rows = pl.ds(pl.multiple_of(kj * tq, tq), tq)
        bias = mask_ref[qi - kj]
        new = []
        pair = lambda h: slice((h // 2) * LANES, (h // 2 + 1) * LANES)
        score = lambda h: _dot_nt(qms[h], k_ref[rows, pair(h)])
        s_next = score(0)
        for h in range(N_HEADS):
            sl = pair(h)
            m, l, acc = carry[h]
            s = s_next
            if h + 1 < N_HEADS:
                s_next = score(h + 1)
            sm = s + bias
            m_new = jnp.maximum(m, jnp.max(sm, axis=1, keepdims=True))
            alpha = jnp.exp(m - m_new)
            p = jnp.exp(sm - m_new)
            l = alpha * l + jnp.sum(p, axis=1, keepdims=True)
            acc = alpha * acc + _dot(p, v_ref[rows, sl])
            new.append((m_new, l, acc))
        return tuple(new)

    init = tuple((jnp.full((tq, 1), NEG_BIG, F32), jnp.zeros((tq, 1), F32), jnp.zeros((tq, LANES), F32))
                 for _ in range(N_HEADS))
    res = lax.fori_loop(0, qi + 1, body, init)
    for hp in range(HEAD_PAIRS):
        (_, l0, a0), (_, l1, a1) = res[2 * hp], res[2 * hp + 1]
        o_ref[:, hp * LANES:(hp + 1) * LANES] = jnp.where(first, a0 / l0, a1 / l1)


def _attn_prompt(q, k, qkv, batch, seq, tq=512):
    nb = seq // tq
    pos = np.arange(tq)
    with np.errstate(divide="ignore"):
        masks = np.log(np.stack([_multiplicity(pos + d * tq, pos) for d in range(nb)]))
    return pl.pallas_call(
        functools.partial(_attn_prompt_kernel, tq=tq),
        grid=(batch, nb),
        in_specs=[pl.BlockSpec((tq, D_ATT), lambda b, i: (b * nb + i, 0)),
                  pl.BlockSpec((seq, D_ATT), lambda b, i: (b, 0)),
                  pl.BlockSpec((seq, D_ATT), lambda b, i: (b, 2)),
                  pl.BlockSpec((nb, tq, tq), lambda b, i: (0, 0, 0))],
        out_specs=pl.BlockSpec((tq, D_ATT), lambda b, i: (b * nb + i, 0)),
        out_shape=jax.ShapeDtypeStruct((batch * seq, D_ATT), F32),
        compiler_params=_cparams(("parallel", "arbitrary")),
        name="attn_prompt",
    )(q, k, qkv, jnp.asarray(masks))


def _attn_sample_kernel(q_ref, kn_ref, vn_ref, kt_ref, vt_ref, mp_ref, mn_ref, hm_ref, o_ref, *, tq):
    q = q_ref[...]
    hm = hm_ref[...]
    qbd = jnp.concatenate([q] * N_HEADS, axis=0) * hm
    s_past = _dot(qbd, kt_ref[...])
    s_new = _dot_nt(qbd, kn_ref[...])
    c_past, c_new = mp_ref[...], mn_ref[...]
    sm_past = jnp.where(c_past > 0.0, s_past, NEG_BIG)
    sm_new = jnp.where(c_new > 0.0, s_new, NEG_BIG)
    m = jnp.maximum(jnp.max(sm_past, axis=1, keepdims=True), jnp.max(sm_new, axis=1, keepdims=True))
    p_past = jnp.exp(sm_past - m) * c_past
    p_new = jnp.exp(sm_new - m) * c_new
    l = jnp.sum(p_past, axis=1, keepdims=True) + jnp.sum(p_new, axis=1, keepdims=True)
    o = (_dot_nt(p_past, vt_ref[...]) + _dot(p_new, vn_ref[...])) / l * hm
    out = o[0:tq]
    for h in range(1, N_HEADS):
        out = out + o[h * tq:(h + 1) * tq]
    o_ref[...] = out


def _attn_sample(q, k, qkv, row0, kt_all, vt_all, layer, batch, tq, past):
    q_pos = past + np.arange(tq)
    rep = lambda c: np.concatenate([c] * N_HEADS, axis=0)
    mp = rep(_multiplicity(q_pos, np.arange(past)))
    mn = rep(_multiplicity(q_pos, q_pos))
    row_head = np.arange(N_HEADS * tq) // tq
    hm = (row_head[:, None] == (np.arange(D_ATT) // HEAD_DIM)[None, :]).astype(np.float32)
    rb0 = row0 // tq
    new_spec = pl.BlockSpec((tq, D_ATT), lambda b: (b, 0))
    cache_spec = pl.BlockSpec((None, None, D_ATT, past), lambda b: (layer, b, 0, 0))
    full_spec = lambda a: pl.BlockSpec(a.shape, lambda b: (0,) * a.ndim)
    mp, mn, hm = jnp.asarray(mp), jnp.asarray(mn), jnp.asarray(hm)
    return pl.pallas_call(
        functools.partial(_attn_sample_kernel, tq=tq),
        grid=(batch,),
        in_specs=[new_spec, new_spec, pl.BlockSpec((tq, D_ATT), lambda b: (rb0 + b, 2)),
                  cache_spec, cache_spec, full_spec(mp), full_spec(mn), full_spec(hm)],
        out_specs=new_spec,
        out_shape=jax.ShapeDtypeStruct((batch * tq, D_ATT), F32),
        compiler_params=_cparams(("parallel",)),
        name="attn_sample",
    )(q, k, qkv, kt_all, vt_all, mp, mn, hm)


def _rglru_kernel(x_ref, g_ref, tail0_ref, h0_ref, cw_ref, cb_ref, wa_ref, ba_ref, wx_ref, bx_ref, lam_ref,
                  y_ref, tail_ref, hl_ref, xpad, a_s, b_s, hc, *, batch, tt):
    rows = tt * batch
    pad = (CONV_W - 1) * batch
    i = pl.program_id(0)

    @pl.when(i == 0)
    def _():
        xpad[0:pad, :] = tail0_ref[...]
        hc[...] = h0_ref[...]

    @pl.when(i > 0)
    def _():
        xpad[0:pad, :] = xpad[rows:rows + pad, :]

    xpad[pad:pad + rows, :] = x_ref[...]
    xc = cb_ref[...] + cw_ref[0:1, :] * xpad[0:rows, :]
    for j in range(1, CONV_W):
        xc = xc + cw_ref[j:j + 1, :] * xpad[j * batch:j * batch + rows, :]
    r = jax.nn.sigmoid(_dot(xc, wa_ref[...]) + ba_ref[...])
    ig = jax.nn.sigmoid(_dot(xc, wx_ref[...]) + bx_ref[...])
    log_a = (-LRU_C) * r * _softplus(-lam_ref[...])
    a = jnp.exp(log_a)
    a_s[...] = a
    b_s[...] = jnp.sqrt(-jnp.tanh(log_a) * (a * a + 1.0)) * (ig * xc)

    def step(t, h):
        sl = pl.ds(pl.multiple_of(t * batch, batch), batch)
        h = a_s[sl, :] * h + b_s[sl, :]
        a_s[sl, :] = h
        return h

    h = lax.fori_loop(0, tt, step, hc[...])
    hc[...] = h
    y_ref[...] = a_s[...] * jax.nn.gelu(g_ref[...])
    tail_ref[...] = xpad[rows:rows + pad, :]
    hl_ref[...] = h


def _rglru_seq_kernel(x_ref, g_ref, tail0_ref, h0_ref, cw_ref, cb_ref, wa_ref, ba_ref, wx_ref, bx_ref, lam_ref,
                      y_ref, tail_ref, hl_ref, xprev, hc, *, tt):
    @pl.when(pl.program_id(1) == 0)
    def _():
        xprev[...] = tail0_ref[...]
        hc[...] = h0_ref[...]

    x = x_ref[...]
    ext = jnp.concatenate([xprev[...], x], axis=0)
    xc = cb_ref[...] + cw_ref[CONV_W - 1:CONV_W, :] * x
    for j in range(1, CONV_W):
        xc = xc + cw_ref[CONV_W - 1 - j:CONV_W - j, :] * pltpu.roll(ext, j, 0)[SUBLANES:]
    xprev[...] = x[tt - SUBLANES:]
    r = jax.nn.sigmoid(_dot(xc, wa_ref[...]) + ba_ref[...])
    ig = jax.nn.sigmoid(_dot(xc, wx_ref[...]) + bx_ref[...])
    log_a = (-LRU_C) * r * _softplus(-lam_ref[...])
    a = jnp.exp(log_a)
    b = jnp.sqrt(-jnp.tanh(log_a) * (a * a + 1.0)) * (ig * xc)
    row = lax.broadcasted_iota(jnp.int32, (tt, 1), 0)
    s = 1
    while s < tt:
        keep = row >= s
        b = a * jnp.where(keep, pltpu.roll(b, s, 0), 0.0) + b
        a = a * jnp.where(keep, pltpu.roll(a, s, 0), 1.0)
        s *= 2
    h = a * hc[...] + b
    hc[...] = h[tt - 1:tt]
    y_ref[...] = h * jax.nn.gelu(g_ref[...])
    tail_ref[...] = x[tt - SUBLANES:]
    hl_ref[...] = h[tt - 1:tt]


def _rglru_seq(rnn2d, conv, h0, p, batch, seq, tt=256):
    pad = SUBLANES - (CONV_W - 1)
    tail0 = jnp.pad(conv, ((0, 0), (pad, 0), (0, 0)))
    const = lambda shape: pl.BlockSpec(shape, lambda b, i: (0,) * len(shape))
    vec = const((1, D_ATT))
    per_seq = lambda rows: pl.BlockSpec((None, rows, D_ATT), lambda b, i: (b, 0, 0))
    y, tail, h_last = pl.pallas_call(
        functools.partial(_rglru_seq_kernel, tt=tt),
        grid=(batch, seq // tt),
        in_specs=[pl.BlockSpec((tt, D_ATT), lambda b, i: (i, 2 * b)),
                  pl.BlockSpec((tt, D_ATT), lambda b, i: (i, 2 * b + 1)),
                  per_seq(SUBLANES), per_seq(1), const((CONV_W, D_ATT)), vec,
                  const((D_ATT, D_ATT)), vec, const((D_ATT, D_ATT)), vec, vec],
        out_specs=[pl.BlockSpec((tt, D_ATT), lambda b, i: (i, b)), per_seq(SUBLANES), per_seq(1)],
        out_shape=[jax.ShapeDtypeStruct((seq, batch * D_ATT), F32),
                   jax.ShapeDtypeStruct((batch, SUBLANES, D_ATT), F32),
                   jax.ShapeDtypeStruct((batch, 1, D_ATT), F32)],
        scratch_shapes=[pltpu.VMEM((SUBLANES, D_ATT), F32), pltpu.VMEM((1, D_ATT), F32)],
        compiler_params=_cparams(("parallel", "arbitrary")),
        name="rglru_seq",
    )(rnn2d, rnn2d, tail0, h0.reshape(batch, 1, D_ATT), p["conv_w"], p["conv_b"].reshape(1, D_ATT),
      _block_diag(p["rg_wa"]).astype(BF16), p["rg_ba"].reshape(1, D_ATT),
      _block_diag(p["rg_wx"]).astype(BF16), p["rg_bx"].reshape(1, D_ATT), p["rg_lambda"].reshape(1, D_ATT))
    return y, tail[:, pad:, :], h_last.reshape(batch, D_ATT)


def _block_diag(w):
    nb, n, _ = w.shape
    eye = jnp.eye(nb, dtype=w.dtype)
    return (w[:, :, None, :] * eye[:, None, :, None]).reshape(nb * n, nb * n)


def _rglru(rnn_tm, tail0, h0, p, batch, seq, tt):
    rows = tt * batch
    pad = (CONV_W - 1) * batch
    row_spec = pl.BlockSpec((rows, D_ATT), lambda i: (i, 0))
    const = lambda shape: pl.BlockSpec(shape, lambda i: (0,) * len(shape))
    vec = const((1, D_ATT))
    x_tm = g_tm = rnn_tm
    return pl.pallas_call(
        functools.partial(_rglru_kernel, batch=batch, tt=tt),
        grid=(seq // tt,),
        in_specs=[row_spec, pl.BlockSpec((rows, D_ATT), lambda i: (i, 1)), const((pad, D_ATT)),
                  const((batch, D_ATT)), const((CONV_W, D_ATT)), vec,
                  const((D_ATT, D_ATT)), vec, const((D_ATT, D_ATT)), vec, vec],
        out_specs=[row_spec, const((pad, D_ATT)), const((batch, D_ATT))],
        out_shape=[jax.ShapeDtypeStruct((seq * batch, D_ATT), F32),
                   jax.ShapeDtypeStruct((pad, D_ATT), F32),
                   jax.ShapeDtypeStruct((batch, D_ATT), F32)],
        scratch_shapes=[pltpu.VMEM((pad + rows, D_ATT), F32), pltpu.VMEM((rows, D_ATT), F32),
                        pltpu.VMEM((rows, D_ATT), F32), pltpu.VMEM((batch, D_ATT), F32)],
        compiler_params=_cparams(("arbitrary",)),
        name="rglru",
    )(x_tm, g_tm, tail0, h0, p["conv_w"], p["conv_b"].reshape(1, D_ATT),
      _block_diag(p["rg_wa"]).astype(BF16), p["rg_ba"].reshape(1, D_ATT),
      _block_diag(p["rg_wx"]).astype(BF16), p["rg_bx"].reshape(1, D_ATT), p["rg_lambda"].reshape(1, D_ATT))


def _rwkv_token_math(c, prev, mu, w0, w2, a0, a2, g2, kk_w, ka_w, ones):
    cm = c + (prev - c) * mu
    r = cm[:, 0:D_ATT]
    k = cm[:, D_ATT:2 * D_ATT]
    v = cm[:, 2 * D_ATT:3 * D_ATT]
    lo = cm[:, 3 * D_ATT:3 * D_ATT + LORA_W + LORA_A]
    g_lo = cm[:, 3 * D_ATT + LORA_W + LORA_A:]
    w = -_softplus(-(w0 + _dot(jnp.tanh(lo), w2))) - 0.5
    a = jax.nn.sigmoid(a0 + _dot(lo, a2))
    g = _dot(jax.nn.sigmoid(g_lo), g2)
    kk = k * kk_w
    nrm = jnp.sqrt(_dot_split(kk * kk, ones, 2))
    return r, k * (1.0 + (a - 1.0) * ka_w), v, kk / jnp.maximum(nrm, 1e-12), a, -jnp.exp(w), g


def _rwkv_pre_kernel(c_ref, prev_ref, mu_ref, w0_ref, w2_ref, a0_ref, a2_ref, g2_ref, kk_ref, ka_ref, ones_ref,
                     *out_refs):
    vals = _rwkv_token_math(c_ref[...], prev_ref[...], mu_ref[...], w0_ref[...], w2_ref[...], a0_ref[...],
                            a2_ref[...], g2_ref[...], kk_ref[...], ka_ref[...], ones_ref[...])
    for ref, val in zip(out_refs, vals):
        ref[...] = val


def _rwkv_token_params(p):
    w2p = jnp.concatenate([p["rwkv_w2"], jnp.zeros((LORA_A, D_ATT), F32)], axis=0).astype(BF16)
    a2p = jnp.concatenate([jnp.zeros((LORA_W, D_ATT), F32), p["rwkv_a2"]], axis=0).astype(BF16)
    return (p["rwkv_mu"].reshape(1, C_COLS), p["rwkv_w0"].reshape(1, D_ATT), w2p, p["rwkv_a0"].reshape(1, D_ATT),
            a2p, p["rwkv_g2"].astype(BF16), p["rwkv_kk"].reshape(1, D_ATT), p["rwkv_ka"].reshape(1, D_ATT),
            _head_ones())


def _rwkv_pre(cc, prev, p, tm=256):
    n = cc.shape[0]
    params = _rwkv_token_params(p)
    row = pl.BlockSpec((tm, C_COLS), lambda i: (i, 0))
    const = lambda a: pl.BlockSpec(a.shape, lambda i: (0,) * a.ndim)
    out = pl.BlockSpec((tm, D_ATT), lambda i: (i, 0))
    return pl.pallas_call(
        _rwkv_pre_kernel,
        grid=(n // tm,),
        in_specs=[row, row] + [const(a) for a in params],
        out_specs=[out] * 7,
        out_shape=[jax.ShapeDtypeStruct((n, D_ATT), F32)] * 7,
        compiler_params=_cparams(("parallel",)),
        name="rwkv_pre",
    )(cc, prev, *params)


def _wkv_kernel(*refs, c, bb):
    c_refs = refs[:bb]
    (shift0_ref, s0_ref, mu_ref, w0_ref, w2_ref, a0_ref, a2_ref, g2_ref, kk_ref, ka_ref, ones_ref,
     rk_ref, lnw_ref, lnb_ref, tri_ref, sl_ref, il_ref, eye_ref, bm_ref,
     o_ref, sl_out_ref, s_scr, y_scr, last_scr) = refs[bb:]
    ci = pl.program_id(1)

    @pl.when(ci == 0)
    def _():
        s_scr[...] = s0_ref[...]
        last_scr[...] = shift0_ref[...]

    rk, lnw, lnb, tri, strict, incl, eye, bmask, ones = (
        rk_ref[...], lnw_ref[...], lnb_ref[...], tri_ref[...], sl_ref[...], il_ref[...], eye_ref[...],
        bm_ref[...], ones_ref[...])
    token_params = (mu_ref[...], w0_ref[...], w2_ref[...], a0_ref[...], a2_ref[...], g2_ref[...], kk_ref[...],
                    ka_ref[...], ones)
    first_row = lax.broadcasted_iota(jnp.int32, (bb * c, 1), 0) % c == 0
    lane = lax.broadcasted_iota(jnp.int32, (1, LANES), 1)
    first = lane < HEAD_DIM
    stack = lambda z: jnp.concatenate([jnp.where(first, z, 0.0), jnp.where(first, 0.0, z)], axis=0)
    twice = lambda z: jnp.concatenate([z, z], axis=0)
    fold = lambda z: z[:c] + z[c:]
    n_fac = int(np.log2(c))
    two_c = 2 * c

    cols = jnp.concatenate([c_refs[bi][0] for bi in range(bb)], axis=0)
    before = jnp.concatenate([jnp.broadcast_to(last_scr[bi], (c, C_COLS)) for bi in range(bb)], axis=0)
    prev = jnp.where(first_row, before, pltpu.roll(cols, 1, 0))
    for bi in range(bb):
        last_scr[bi] = cols[(bi + 1) * c - 1:(bi + 1) * c, :]
    tokens = _rwkv_token_math(cols, prev, *token_params)
    cum_all = _split_dot_left(tri, tokens[5], 3)
    seqs = []
    for bi in range(bb):
        r, k, v, kap, a, lw, g = (z[bi * c:(bi + 1) * c] for z in tokens)
        cum = cum_all[bi * c:(bi + 1) * c]
        tot = cum[c - 1:c, :]
        e_neg, e_end = jnp.exp(-cum), jnp.exp(tot - cum)
        b = kap * a
        seqs.append(dict(r=r, k=k, v=v, g=g, kap_t=kap * jnp.exp(cum - lw), r_t=r * jnp.exp(cum), k_t=k * e_neg,
                         b_t=b * e_neg, k_h=k * e_end, b_h=b * e_end, g_end=jnp.exp(tot)))
    chains = [(bi, hp, slice(hp * LANES, (hp + 1) * LANES)) for bi in range(bb) for hp in range(HEAD_PAIRS)]

    grams, states, v_stacks = [], [], []
    for bi, hp, sl in chains:
        q = seqs[bi]
        grams.append(_dot_nt(jnp.concatenate([stack(q["kap_t"][:, sl]), stack(q["r_t"][:, sl])], axis=0),
                             jnp.concatenate([twice(q["k_t"][:, sl]), twice(q["b_t"][:, sl])], axis=0)))
        states.append(s_scr[bi, hp])
        v_stacks.append(stack(q["v"][:, sl]))
    from_state = [_dot_nt(jnp.concatenate([seqs[bi]["kap_t"][:, sl], seqs[bi]["r_t"][:, sl]], axis=0), s_p)
                  for (bi, hp, sl), s_p in zip(chains, states)]
    a_kk = [gm[:two_c, :two_c] * strict for gm in grams]
    y_mat = [jnp.concatenate([gm[two_c:, :two_c] * incl, -(gm[two_c:, two_c:] * incl)], axis=1) for gm in grams]
    pw = [-(gm[:two_c, two_c:] * strict) for gm in grams]
    inv = [eye + m for m in pw]
    for _ in range(n_fac - 1):
        pw = [_dot(m, m) for m in pw]
        inv = [t + _dot(t, m) for t, m in zip(inv, pw)]
    x_s = [stack(fs[:c]) + _dot(am, vs) for fs, am, vs in zip(from_state, a_kk, v_stacks)]
    u_s = [_dot(t, x) for t, x in zip(inv, x_s)]
    y_s = [stack(fs[c:]) + _dot(ym, jnp.concatenate([vs, u], axis=0))
           for fs, ym, vs, u in zip(from_state, y_mat, v_stacks, u_s)]
    upd = [_dot(jnp.concatenate([seqs[bi]["v"][:, sl], fold(u)], axis=0).T,
                jnp.concatenate([seqs[bi]["k_h"][:, sl], -seqs[bi]["b_h"][:, sl]], axis=0))
           for (bi, hp, sl), u in zip(chains, u_s)]
    for (bi, hp, sl), s_p, d_s, y in zip(chains, states, upd, y_s):
        s_scr[bi, hp] = s_p * seqs[bi]["g_end"][:, sl] + bmask * d_s
        y_scr[bi * c:(bi + 1) * c, sl] = fold(y)

    r, k, v, g = tokens[0], tokens[1], tokens[2], tokens[6]
    inv_n = 1.0 / HEAD_DIM
    y = y_scr[...]
    mean = _dot_split(y, ones, 2) * inv_n
    d = y - mean
    var = _dot_split(d * d, ones, 2) * inv_n
    yn = d * lax.rsqrt(var + GN_EPS) * lnw + lnb
    out = (yn + _dot_split(r * k * rk, ones, 2) * v) * g
    for bi in range(bb):
        o_ref[bi, 0] = out[bi * c:(bi + 1) * c]

    @pl.when(ci == pl.num_programs(1) - 1)
    def _():
        sl_out_ref[...] = s_scr[...]


def _wkv(cc, shift0, s0_pairs, p, batch, seq, c=64, bb=4):
    assert 2 * c == LANES
    nc = seq // c
    cc3 = cc.reshape(batch * nc, c, C_COLS)
    token_params = _rwkv_token_params(p)
    idx = np.arange(2 * c)
    same = (idx[:, None] // c) == (idx[None, :] // c)
    strict = (same & (idx[None, :] % c < idx[:, None] % c)).astype(np.float32)
    incl = (same & (idx[None, :] % c <= idx[:, None] % c)).astype(np.float32)
    eye = np.eye(2 * c, dtype=np.float32)
    half = np.arange(LANES) // HEAD_DIM
    bmask = (half[:, None] == half[None, :]).astype(np.float32)
    rows = np.arange(bb * c)
    tri = ((rows[None, :] <= rows[:, None]) & (rows[None, :] // c == rows[:, None] // c)).astype(np.float32)

    def chunk(j):
        return pl.BlockSpec((1, c, C_COLS), lambda g, i: ((g * bb + j) * nc + i, 0, 0))

    const = lambda a: pl.BlockSpec(a.shape, lambda g, i: (0,) * a.ndim)
    state = pl.BlockSpec((bb, HEAD_PAIRS, LANES, LANES), lambda g, i: (g, 0, 0, 0))
    consts = token_params + (p["rwkv_rk"].reshape(1, D_ATT), p["rwkv_lnw"].reshape(1, D_ATT),
                             p["rwkv_lnb"].reshape(1, D_ATT), jnp.asarray(tri, dtype=BF16), jnp.asarray(strict),
                             jnp.asarray(incl), jnp.asarray(eye), jnp.asarray(bmask))
    return pl.pallas_call(
        functools.partial(_wkv_kernel, c=c, bb=bb),
        grid=(batch // bb, nc),
        in_specs=[chunk(j) for j in range(bb)]
        + [pl.BlockSpec((bb, 1, C_COLS), lambda g, i: (g, 0, 0)), state] + [const(a) for a in consts],
        out_specs=[pl.BlockSpec((bb, 1, c, D_ATT), lambda g, i: (g, i, 0, 0)), state],
        out_shape=[jax.ShapeDtypeStruct((batch, nc, c, D_ATT), F32),
                   jax.ShapeDtypeStruct((batch, HEAD_PAIRS, LANES, LANES), F32)],
        scratch_shapes=[pltpu.VMEM((bb, HEAD_PAIRS, LANES, LANES), F32), pltpu.VMEM((bb * c, D_ATT), F32),
                        pltpu.VMEM((bb, 1, C_COLS), F32)],
        compiler_params=_cparams(("parallel", "arbitrary")),
        name="wkv",
    )(*([cc3] * bb), shift0.reshape(batch, 1, C_COLS), s0_pairs, *consts)


def _wkv_lane_kernel(r_ref, k_ref, v_ref, kap_ref, a_ref, lw_ref, g_ref, s0_ref, rk_ref, lnw_ref, lnb_ref,
                     o_ref, s_ref, *, steps):
    s_ref[...] = s0_ref[...]
    inv_n = 1.0 / HEAD_DIM
    for t in range(steps):
        kap = kap_ref[t]
        w = jnp.exp(lw_ref[t])
        bt = kap * a_ref[t]
        kt = k_ref[t]
        rt = r_ref[t]

        def per_v(vi, _, t=t, kap=kap, w=w, bt=bt, kt=kt, rt=rt):
            s = s_ref[vi]
            sa = -jnp.sum(s * kap, axis=0, keepdims=True)
            vv = v_ref[t, pl.ds(vi, 1), :]
            s = s * w + sa * bt + vv * kt
            s_ref[vi] = s
            o_ref[t, pl.ds(vi, 1), :] = jnp.sum(s * rt, axis=0, keepdims=True)
            return 0

        lax.fori_loop(0, HEAD_DIM, per_v, 0, unroll=4)
        y = o_ref[t]
        mean = jnp.sum(y, axis=0, keepdims=True) * inv_n
        d = y - mean
        var = jnp.sum(d * d, axis=0, keepdims=True) * inv_n
        yn = d * lax.rsqrt(var + GN_EPS) * lnw_ref[...] + lnb_ref[...]
        bonus = jnp.sum(rt * kt * rk_ref[...], axis=0, keepdims=True) * v_ref[t]
        o_ref[t] = (yn + bonus) * g_ref[t]


def _wkv_lane(parts_t, s0_t, p, batch, steps):
    tok = pl.BlockSpec((steps, HEAD_DIM, batch), lambda h: (0, h, 0))
    st = pl.BlockSpec((HEAD_DIM, HEAD_DIM, batch), lambda h: (h, 0, 0))
    col = pl.BlockSpec((HEAD_DIM, batch), lambda h: (h, 0))
    bcast = lambda z: jnp.broadcast_to(z.reshape(D_ATT, 1), (D_ATT, batch))
    return pl.pallas_call(
        functools.partial(_wkv_lane_kernel, steps=steps),
        grid=(N_HEADS,),
        in_specs=[tok] * 7 + [st, col, col, col],
        out_specs=[tok, st],
        out_shape=[jax.ShapeDtypeStruct((steps, D_ATT, batch), F32),
                   jax.ShapeDtypeStruct((D_ATT, HEAD_DIM, batch), F32)],
        compiler_params=_cparams(("parallel",)),
        name="wkv_lane",
    )(*parts_t, s0_t, bcast(p["rwkv_rk"]), bcast(p["rwkv_lnw"]), bcast(p["rwkv_lnb"]))


def _pair_states(s):
    b = s.shape[0]
    s = s.reshape(b, HEAD_PAIRS, 2, HEAD_DIM, HEAD_DIM)
    z = jnp.zeros((b, HEAD_PAIRS, HEAD_DIM, HEAD_DIM), s.dtype)
    top = jnp.concatenate([s[:, :, 0], z], axis=-1)
    bot = jnp.concatenate([z, s[:, :, 1]], axis=-1)
    return jnp.concatenate([top, bot], axis=-2)


def _unpair_states(sp):
    b = sp.shape[0]
    first = sp[:, :, :HEAD_DIM, :HEAD_DIM]
    second = sp[:, :, HEAD_DIM:, HEAD_DIM:]
    return jnp.stack([first, second], axis=2).reshape(b, N_HEADS, HEAD_DIM, HEAD_DIM)


def _merge_kernel(oa_ref, ob_ref, oc_ref, gl_ref, x_ref, bm_ref, wa_ref, wb_ref, wc_ref, wo_ref, nw_ref,
                  wgh_ref, wgl_ref, bg_ref, x1_ref, xn_ref, comb_ref):
    merged = None
    for j, (o_ref, w_ref) in enumerate(((oa_ref, wa_ref), (ob_ref, wb_ref), (oc_ref, wc_ref))):
        cols = slice(j * D_MODEL, (j + 1) * D_MODEL)
        gate = jax.nn.sigmoid(gl_ref[:, cols] + bm_ref[:, cols])
        t = gate * _dot(o_ref[...], w_ref[...])
        merged = t if merged is None else merged + t
    x1 = x_ref[...] + _dot(merged, wo_ref[...])
    x1_ref[...] = x1
    xn = x1 * lax.rsqrt(jnp.mean(x1 * x1, axis=-1, keepdims=True) + EPS) * nw_ref[...]
    xn_ref[...] = xn.astype(BF16)
    hi = xn.astype(BF16)
    lo = (xn - hi.astype(F32)).astype(BF16)
    wgh, wgl = wgh_ref[...], wgl_ref[...]
    lg = (jnp.dot(hi, wgh, preferred_element_type=F32) + jnp.dot(lo, wgh, preferred_element_type=F32)
          + jnp.dot(hi, wgl, preferred_element_type=F32)) + bg_ref[...]
    lane = lax.broadcasted_iota(jnp.int32, lg.shape, 1).astype(F32)
    none = float(LANES)
    row_max = lambda z: jnp.max(z, axis=1, keepdims=True)
    first_lane = lambda hit: jnp.min(jnp.where(hit, lane, none), axis=1, keepdims=True)
    g_logit = jnp.where(lane < N_GROUPS, lg, -jnp.inf)
    best = row_max(g_logit)
    gi = first_lane(g_logit == best)
    p_group = 1.0 / jnp.sum(jnp.exp(g_logit - best), axis=1, keepdims=True)
    base = N_GROUPS + gi * EXP_PER_GROUP
    e_logit = jnp.where(lane >= base, jnp.where(lane < base + EXP_PER_GROUP, lg, -jnp.inf), -jnp.inf)
    v1 = row_max(e_logit)
    l1 = first_lane(e_logit == v1)
    rest = jnp.where(lane == l1, -jnp.inf, e_logit)
    v2 = row_max(rest)
    l2 = first_lane(rest == v2)
    e2 = jnp.exp(v2 - v1)
    w1 = p_group / (1.0 + e2)
    w2 = p_group * e2 / (1.0 + e2)
    comb_ref[...] = jnp.where(lane == l1 - N_GROUPS, w1, 0.0) + jnp.where(lane == l2 - N_GROUPS, w2, 0.0)


def _merge(oa, ob, oc, gl, x, p, wb, tm=512, ob_time_major=None):
    n = x.shape[0]
    ob_spec = pl.BlockSpec((tm, D_ATT), lambda i: (i, 0))
    if ob_time_major is not None:
        batch, seq = ob_time_major
        nt = seq // tm
        ob = ob.reshape(seq, batch * D_ATT)
        ob_spec = pl.BlockSpec((tm, D_ATT), lambda i: (i % nt, i // nt))
    wg = jnp.concatenate([p["moe_wg_group"], p["moe_wg_exp"]], axis=1)
    wg = jnp.pad(wg, ((0, 0), (0, LANES - wg.shape[1])))
    wgh = wg.astype(BF16)
    wgl = (wg - wgh.astype(F32)).astype(BF16)
    bg = jnp.pad(jnp.concatenate([p["moe_bg_group"], p["moe_bg_exp"]]), (0, LANES - N_GROUPS - N_EXPERTS))
    row = lambda w: pl.BlockSpec((tm, w), lambda i: (i, 0))
    const = lambda shape: pl.BlockSpec(shape, lambda i: (0,) * len(shape))
    return pl.pallas_call(
        _merge_kernel,
        grid=(n // tm,),
        in_specs=[row(D_ATT), ob_spec, row(D_ATT), row(GL_COLS), row(D_MODEL), const((1, GL_COLS)),
                  const((D_ATT, D_MODEL)), const((D_ATT, D_MODEL)), const((D_ATT, D_MODEL)),
                  const((D_MODEL, D_MODEL)), const((1, D_MODEL)), const((D_MODEL, LANES)),
                  const((D_MODEL, LANES)), const((1, LANES))],
        out_specs=[row(D_MODEL), row(D_MODEL), row(LANES)],
        out_shape=[jax.ShapeDtypeStruct((n, D_MODEL), F32), jax.ShapeDtypeStruct((n, D_MODEL), BF16),
                   jax.ShapeDtypeStruct((n, LANES), F32)],
        compiler_params=_cparams(("parallel",)),
        name="merge",
    )(oa, ob, oc, gl, x, p["b_merge"].reshape(1, GL_COLS), wb["w_branch_a"], wb["w_branch_b"], wb["w_branch_c"],
      wb["w_out"], p["norm2_w"].reshape(1, D_MODEL), wgh, wgl, bg.reshape(1, LANES))


MOE_PAIR = 2


def _moe_kernel(xn_ref, comb_ref, x1_ref, w1_ref, w3_ref, w2_ref, o_ref, acc):
    e = pl.program_id(1)

    @pl.when(e == 0)
    def _():
        acc[...] = x1_ref[...]

    xb = xn_ref[...]
    lane = lax.broadcasted_iota(jnp.int32, comb_ref.shape, 1)
    comb = comb_ref[...]
    pick = lambda j: jnp.sum(jnp.where(lane == MOE_PAIR * e + j, comb, 0.0), axis=1, keepdims=True)
    hidden = []
    for j in range(MOE_PAIR):
        h = jax.nn.silu(jnp.dot(xb, w1_ref[j], preferred_element_type=F32)) * jnp.dot(xb, w3_ref[j],
                                                                                      preferred_element_type=F32)
        hidden.append((h * pick(j)).astype(BF16))
    acc[...] += jnp.dot(jnp.concatenate(hidden, axis=1), w2_ref[0], preferred_element_type=F32)

    @pl.when(e == pl.num_programs(1) - 1)
    def _():
        o_ref[...] = acc[...]


def _moe_weights(w1, w3, w2):
    return (w1.astype(BF16), w3.astype(BF16),
            w2.astype(BF16).reshape(N_EXPERTS // MOE_PAIR, MOE_PAIR * D_EXPERT, D_MODEL))


def _moe(xn, comb, x1, w1, w3, w2, tm=1024):
    n = xn.shape[0]
    hid = MOE_PAIR * D_EXPERT
    return pl.pallas_call(
        _moe_kernel,
        grid=(n // tm, N_EXPERTS // MOE_PAIR),
        in_specs=[pl.BlockSpec((tm, D_MODEL), lambda i, e: (i, 0)),
                  pl.BlockSpec((tm, LANES), lambda i, e: (i, 0)),
                  pl.BlockSpec((tm, D_MODEL), lambda i, e: (i, 0)),
                  pl.BlockSpec((MOE_PAIR, D_MODEL, D_EXPERT), lambda i, e: (e, 0, 0)),
                  pl.BlockSpec((MOE_PAIR, D_MODEL, D_EXPERT), lambda i, e: (e, 0, 0)),
                  pl.BlockSpec((1, hid, D_MODEL), lambda i, e: (e, 0, 0))],
        out_specs=pl.BlockSpec((tm, D_MODEL), lambda i, e: (i, 0)),
        out_shape=jax.ShapeDtypeStruct((n, D_MODEL), F32),
        scratch_shapes=[pltpu.VMEM((tm, D_MODEL), F32)],
        compiler_params=_cparams(("parallel", "arbitrary")),
        name="moe",
    )(xn, comb, x1, w1, w3, w2)


def _time_major(z, batch, seq):
    return z.reshape(batch, seq, z.shape[-1]).transpose(1, 0, 2).reshape(seq * batch, z.shape[-1])


def _batch_major(z, batch, seq):
    return z.reshape(seq, batch, z.shape[-1]).transpose(1, 0, 2).reshape(batch * seq, z.shape[-1])


def _group_layer(x, g, p, wb, caches, layer, tabs):
    batch, seq = g["batch"], g["seq"]
    rows = batch * seq
    prompt = g["past"] == 0
    tm = 256
    qkv, rnn, cc, gl = _inproj(x, p["norm1_w"], wb["w_in"], time_major_rnn=(batch, seq) if prompt else None)
    heads = lambda z: z.reshape(batch, seq, N_HEADS, HEAD_DIM)
    if prompt:
        q, k, kt, vt = _qkprep(qkv, 0, rows, tabs, g["table_blocks"], p["q_norm_w"], p["k_norm_w"], tm,
                               cache_batch=batch)
        oa = _attn_prompt(q, k, qkv, batch, seq)
        from_t = lambda z: z.reshape(batch, N_HEADS, HEAD_DIM, seq).transpose(0, 3, 1, 2)
        k_new, v_new = from_t(kt), from_t(vt)
        ob, conv_tail, h_last = _rglru_seq(rnn, g["conv"], g["h"], p, batch, seq)
        o_wkv, s_pairs = _wkv(cc, g["shift"], _pair_states(g["wkv"]), p, batch, seq)
        oc = o_wkv.reshape(rows, D_ATT)
        s_last = _unpair_states(s_pairs)
    else:
        q, k = _qkprep(qkv, 0, rows, tabs, g["table_blocks"], p["q_norm_w"], p["k_norm_w"], tm)
        oa = _attn_sample(q, k, qkv, 0, caches[0], caches[1], layer, batch, seq, g["past"])
        k_new, v_new = heads(k), heads(qkv[:, 2 * D_ATT:])
        tail0 = g["conv"].transpose(1, 0, 2).reshape((CONV_W - 1) * batch, D_ATT)
        y_tm, tail, h_last = _rglru(_time_major(rnn, batch, seq), tail0, g["h"], p, batch, seq, g["tt"])
        ob = _batch_major(y_tm, batch, seq)
        conv_tail = tail.reshape(CONV_W - 1, batch, D_ATT).transpose(1, 0, 2)
        c3 = cc.reshape(batch, seq, C_COLS)
        prev = jnp.concatenate([g["shift"][:, None, :], c3[:, :-1]], axis=1).reshape(rows, C_COLS)
        to_lanes = lambda z: z.reshape(batch, seq, D_ATT).transpose(1, 2, 0)
        o_t, s_t = _wkv_lane([to_lanes(z) for z in _rwkv_pre(cc, prev, p)], g["wkv"], p, batch, seq)
        oc = o_t.transpose(2, 0, 1).reshape(rows, D_ATT)
        s_last = s_t.reshape(N_HEADS, HEAD_DIM, HEAD_DIM, batch).transpose(3, 0, 1, 2)
    shift_last = cc.reshape(batch, seq, C_COLS)[:, -1]
    x1, xn2, comb = _merge(oa, ob, oc, gl, x, p, wb, ob_time_major=(batch, seq) if prompt else None)
    x2 = _moe(xn2, comb, x1, wb["moe_w1"], wb["moe_w3"], wb["moe_w2"], tm=min(1024, rows))
    return x2, (k_new, v_new, conv_tail, h_last, shift_last, s_last)


def kernel(x_prompt, x_sample, cache_k_win, cache_v_win, state_conv, state_h, state_shift, state_wkv, norm1_w, w_in, q_norm_w, k_norm_w, conv_w, conv_b, rg_wa, rg_ba, rg_wx, rg_bx, rg_lambda, rwkv_mu, rwkv_w0, rwkv_w2, rwkv_a0, rwkv_a2, rwkv_g2, rwkv_kk, rwkv_ka, rwkv_rk, rwkv_lnw, rwkv_lnb, b_merge, w_branch_a, w_branch_b, w_branch_c, w_out, norm2_w, moe_wg_group, moe_bg_group, moe_wg_exp, moe_bg_exp, moe_w1, moe_w3, moe_w2):
    weights = dict(norm1_w=norm1_w, w_in=w_in, q_norm_w=q_norm_w, k_norm_w=k_norm_w, conv_w=conv_w, conv_b=conv_b,
                   rg_wa=rg_wa, rg_ba=rg_ba, rg_wx=rg_wx, rg_bx=rg_bx, rg_lambda=rg_lambda, rwkv_mu=rwkv_mu,
                   rwkv_w0=rwkv_w0, rwkv_w2=rwkv_w2, rwkv_a0=rwkv_a0, rwkv_a2=rwkv_a2, rwkv_g2=rwkv_g2,
                   rwkv_kk=rwkv_kk, rwkv_ka=rwkv_ka, rwkv_rk=rwkv_rk, rwkv_lnw=rwkv_lnw, rwkv_lnb=rwkv_lnb,
                   b_merge=b_merge, w_branch_a=w_branch_a, w_branch_b=w_branch_b, w_branch_c=w_branch_c,
                   w_out=w_out, norm2_w=norm2_w, moe_wg_group=moe_wg_group, moe_bg_group=moe_bg_group,
                   moe_wg_exp=moe_wg_exp, moe_bg_exp=moe_bg_exp, moe_w1=moe_w1, moe_w3=moe_w3, moe_w2=moe_w2)
    depth = w_in.shape[0]
    bp, tp = x_prompt.shape[0], x_prompt.shape[1]
    bs, ts = x_sample.shape[0], x_sample.shape[1]
    past = cache_k_win.shape[2]
    x_p = x_prompt.reshape(bp * tp, D_MODEL)
    x_s = x_sample.reshape(bs * ts, D_MODEL)
    tm = 256
    tab_p = _rope_tables(jnp.arange(tp, dtype=jnp.int32))
    tab_s = tuple(jnp.tile(t, (tm // ts, 1)) for t in _rope_tables(past + jnp.arange(ts, dtype=jnp.int32)))
    kt_all = cache_k_win.transpose(0, 1, 3, 4, 2).reshape(depth, bs, D_ATT, past)
    vt_all = cache_v_win.transpose(0, 1, 3, 4, 2).reshape(depth, bs, D_ATT, past)
    wkv_t = state_wkv.transpose(0, 2, 3, 4, 1).reshape(depth, D_ATT, HEAD_DIM, bs)
    outs_p = [[] for _ in range(6)]
    outs_s = [[] for _ in range(6)]
    big = ("w_in", "w_branch_a", "w_branch_b", "w_branch_c", "w_out")
    for l in range(depth):
        p = {k_: v_[l] for k_, v_ in weights.items()}
        wb = {k_: p[k_].astype(BF16) for k_ in big}
        wb["moe_w1"], wb["moe_w3"], wb["moe_w2"] = _moe_weights(p["moe_w1"], p["moe_w3"], p["moe_w2"])
        g_p = dict(batch=bp, seq=tp, past=0, table_blocks=tp // tm, tt=64,
                   conv=jnp.zeros((bp, CONV_W - 1, D_ATT), F32), h=jnp.zeros((bp, D_ATT), F32),
                   shift=jnp.zeros((bp, C_COLS), F32), wkv=jnp.zeros((bp, N_HEADS, HEAD_DIM, HEAD_DIM), F32))
        g_s = dict(batch=bs, seq=ts, past=past, table_blocks=1, tt=ts,
                   conv=state_conv[l], h=state_h[l], shift=state_shift[l], wkv=wkv_t[l])
        x_p, st_p = _group_layer(x_p, g_p, p, wb, None, l, tab_p)
        x_s, st_s = _group_layer(x_s, g_s, p, wb, (kt_all, vt_all), l, tab_s)
        for j in range(6):
            outs_p[j].append(st_p[j])
            outs_s[j].append(st_s[j])
    y_prompt = x_p.reshape(bp, tp, D_MODEL)
    y_sample = x_s.reshape(bs, ts, D_MODEL)
    stack = lambda zs: jnp.stack(zs)
    return (y_prompt, y_sample, *[stack(z) for z in outs_p], *[stack(z) for z in outs_s])
```

```python
import functools

import numpy as np
import jax
import jax.numpy as jnp
from jax import lax
from jax.experimental import pallas as pl
from jax.experimental.pallas import tpu as pltpu

F32 = jnp.float32
BF16 = jnp.bfloat16

D_MODEL = 1024
N_HEADS = 8
HEAD_DIM = 64
D_ATT = N_HEADS * HEAD_DIM
ROT_DIM = HEAD_DIM // 4
ROPE_THETA = 500000.0
PATTERNS = ((128, 1), (512, 4), (2048, 16))
CONV_W = 4
LRU_C = 8.0
LORA_W, LORA_A, LORA_G = 64, 64, 128
C_COLS = 3 * D_ATT + LORA_W + LORA_A + LORA_G
N_BRANCH = 3
GL_COLS = N_BRANCH * D_MODEL
N_GROUPS, EXP_PER_GROUP, N_EXPERTS = 4, 4, 16
D_EXPERT = 512
EPS = 1e-6
GN_EPS = 64e-5

LANES = 128
SUBLANES = 8
VMEM_LIMIT = 56 * 1024 * 1024
HEAD_PAIRS = D_ATT // LANES
NEG_BIG = -1e30


def _cparams(sem):
    return pltpu.CompilerParams(dimension_semantics=sem, vmem_limit_bytes=VMEM_LIMIT)


def _dot(a, b):
    return jnp.dot(a.astype(BF16), b.astype(BF16), preferred_element_type=F32)


def _dot_nt(a, b):
    return lax.dot_general(a.astype(BF16), b.astype(BF16), (((1,), (1,)), ((), ())),
                           preferred_element_type=F32)


def _dot_split(x, w_bf16, passes):
    acc = None
    r = x
    for i in range(passes):
        p = r.astype(BF16)
        t = jnp.dot(p, w_bf16, preferred_element_type=F32)
        acc = t if acc is None else acc + t
        if i + 1 < passes:
            r = r - p.astype(F32)
    return acc


def _split_dot_left(w_exact_bf16, x, passes):
    acc = None
    r = x
    for i in range(passes):
        p = r.astype(BF16)
        t = jnp.dot(w_exact_bf16, p, preferred_element_type=F32)
        acc = t if acc is None else acc + t
        if i + 1 < passes:
            r = r - p.astype(F32)
    return acc


def _softplus(x):
    return jnp.maximum(x, 0.0) + jnp.log1p(jnp.exp(-jnp.abs(x)))


def _head_ones():
    h = np.arange(D_ATT) // HEAD_DIM
    return jnp.asarray((h[:, None] == h[None, :]).astype(np.float32), dtype=BF16)


def _inproj_kernel(x_ref, nw_ref, w_ref, qkv_ref, rg_ref, cc_ref, gl_ref):
    x = x_ref[...]
    xn = x * lax.rsqrt(jnp.mean(x * x, axis=-1, keepdims=True) + EPS) * nw_ref[...]
    xb = xn.astype(BF16)
    off = 0
    for ref in (qkv_ref, rg_ref, cc_ref, gl_ref):
        n = ref.shape[-1]
        ref[...] = jnp.dot(xb, w_ref[:, off:off + n], preferred_element_type=F32)
        off += n


def _inproj(x, norm_w, w_in_bf16, layer, tm=256, time_major_rnn=None):
    n = x.shape[0]
    widths = (3 * D_ATT, 2 * D_ATT, C_COLS, GL_COLS)
    in_cols = sum(widths)
    out_specs = [pl.BlockSpec((tm, w), lambda i: (i, 0)) for w in widths]
    out_shape = [jax.ShapeDtypeStruct((n, w), F32) for w in widths]
    if time_major_rnn is not None:
        batch, seq = time_major_rnn
        nt = seq // tm
        out_specs[1] = pl.BlockSpec((tm, widths[1]), lambda i: (i % nt, i // nt))
        out_shape[1] = jax.ShapeDtypeStruct((seq, batch * widths[1]), F32)
    outs = list(pl.pallas_call(
        _inproj_kernel,
        grid=(n // tm,),
        in_specs=[pl.BlockSpec((tm, D_MODEL), lambda i: (i, 0)),
                  pl.BlockSpec((1, D_MODEL), lambda i: (0, 0)),
                  pl.BlockSpec((None, D_MODEL, in_cols), lambda i: (layer, 0, 0))],
        out_specs=out_specs,
        out_shape=out_shape,
        compiler_params=_cparams(("parallel",)),
        name="inproj",
    )(x, norm_w.reshape(1, D_MODEL), w_in_bf16))
    return outs


def _rope_tables(pos):
    half = ROT_DIM // 2
    inv = ROPE_THETA ** (-jnp.arange(half, dtype=F32) / half)
    ang = pos.astype(F32)[:, None] * inv[None, :]
    cos, sin = jnp.cos(ang), jnp.sin(ang)
    t = pos.shape[0]
    rest = HEAD_DIM - ROT_DIM
    c_head = jnp.concatenate([cos, cos, jnp.ones((t, rest), F32)], axis=1)
    s_prev = jnp.concatenate([jnp.zeros((t, half), F32), sin, jnp.zeros((t, rest), F32)], axis=1)
    s_next = jnp.concatenate([-sin, jnp.zeros((t, half), F32), jnp.zeros((t, rest), F32)], axis=1)
    tile = lambda z: jnp.tile(z, (1, N_HEADS))
    return tile(c_head), tile(s_prev), tile(s_next)


def _qkprep_kernel(q_ref, k_ref, v_ref, cos_ref, sp_ref, sn_ref, qw_ref, kw_ref, ones_ref, qo_ref, ko_ref,
                   *cache_refs):
    half = ROT_DIM // 2
    cos, sp, sn = cos_ref[...], sp_ref[...], sn_ref[...]
    for src, w_ref, dst, scale in ((q_ref, qw_ref, qo_ref, HEAD_DIM ** -0.5), (k_ref, kw_ref, ko_ref, None)):
        x = src[...]
        ms = _dot_split(x * x, ones_ref[...], 2) * (1.0 / HEAD_DIM)
        xn = x * lax.rsqrt(ms + EPS) * w_ref[...]
        y = xn * cos + pltpu.roll(xn, half, 1) * sp + pltpu.roll(xn, D_ATT - half, 1) * sn
        dst[...] = y if scale is None else y * scale
    if cache_refs:
        kt_ref, vt_ref = cache_refs
        kt_ref[...] = ko_ref[...].T
        vt_ref[...] = v_ref[...].T


def _qkprep(qkv, row0, nrows, tables, table_blocks, qw, kw, tm, cache_batch=0):
    cos, sp, sn = tables
    rb0 = row0 // tm
    n_seq = nrows // (tm * table_blocks)
    row = lambda t, b: b * table_blocks + t
    tab_spec = pl.BlockSpec((tm, D_ATT), lambda t, b: (t, 0))
    vec = pl.BlockSpec((1, D_ATT), lambda t, b: (0, 0))
    out_specs = [pl.BlockSpec((tm, D_ATT), lambda t, b: (row(t, b), 0))] * 2
    out_shape = [jax.ShapeDtypeStruct((nrows, D_ATT), F32)] * 2
    if cache_batch:
        out_specs += [pl.BlockSpec((None, D_ATT, tm), lambda t, b: (b, 0, t))] * 2
        out_shape += [jax.ShapeDtypeStruct((cache_batch, D_ATT, table_blocks * tm), F32)] * 2
    return pl.pallas_call(
        _qkprep_kernel,
        grid=(table_blocks, n_seq),
        in_specs=[pl.BlockSpec((tm, D_ATT), lambda t, b: (rb0 + row(t, b), 0)),
                  pl.BlockSpec((tm, D_ATT), lambda t, b: (rb0 + row(t, b), 1)),
                  pl.BlockSpec((tm, D_ATT), lambda t, b: (rb0 + row(t, b), 2)),
                  tab_spec, tab_spec, tab_spec, vec, vec,
                  pl.BlockSpec((D_ATT, D_ATT), lambda t, b: (0, 0))],
        out_specs=out_specs,
        out_shape=out_shape,
        compiler_params=_cparams(("parallel", "parallel")),
        name="qkprep",
    )(qkv, qkv, qkv, cos, sp, sn, jnp.tile(qw, N_HEADS).reshape(1, D_ATT),
      jnp.tile(kw, N_HEADS).reshape(1, D_ATT), _head_ones())


def _multiplicity(q_pos, k_pos):
    d = q_pos[:, None] - k_pos[None, :]
    c = np.zeros(d.shape, np.float32)
    for window, dil in PATTERNS:
        c += ((d >= 0) & (d <= window) & (d % dil == 0) & (k_pos[None, :] >= 0)).astype(np.float32)
    return c


def _attn_prompt_kernel(q_ref, k_ref, v_ref, mask_ref, o_ref, *, tq):
    qi = pl.program_id(1)
    lane = lax.broadcasted_iota(jnp.int32, (1, LANES), 1)
    first = lane < HEAD_DIM
    qms = []
    for hp in range(HEAD_PAIRS):
        qp = q_ref[:, hp * LANES:(hp + 1) * LANES]
        qms.append(jnp.where(first, qp, 0.0).astype(BF16))
        qms.append(jnp.where(first, 0.0, qp).astype(BF16))

    def body(kj, carry):
        rows = pl.ds(pl.multiple_of(kj * tq, tq), tq)
        bias = mask_ref[qi - kj]
        new = []
        pair = lambda h: slice((h // 2) * LANES, (h // 2 + 1) * LANES)
        score = lambda h: _dot_nt(qms[h], k_ref[rows, pair(h)])
        s_next = score(0)
        for h in range(N_HEADS):
            sl = pair(h)
            m, l, acc = carry[h]
            s = s_next
            if h + 1 < N_HEADS:
                s_next = score(h + 1)
            sm = s + bias
            m_new = jnp.maximum(m, jnp.max(sm, axis=1, keepdims=True))
            alpha = jnp.exp(m - m_new)
            p = jnp.exp(sm - m_new)
            l = alpha * l + jnp.sum(p, axis=1, keepdims=True)
            acc = alpha * acc + _dot(p, v_ref[rows, sl])
            new.append((m_new, l, acc))
        return tuple(new)

    init = tuple((jnp.full((tq, 1), NEG_BIG, F32), jnp.zeros((tq, 1), F32), jnp.zeros((tq, LANES), F32))
                 for _ in range(N_HEADS))
    res = lax.fori_loop(0, qi + 1, body, init)
    for hp in range(HEAD_PAIRS):
        (_, l0, a0), (_, l1, a1) = res[2 * hp], res[2 * hp + 1]
        o_ref[:, hp * LANES:(hp + 1) * LANES] = jnp.where(first, a0 / l0, a1 / l1)


def _attn_prompt(q, k, qkv, batch, seq, tq=512):
    nb = seq // tq
    pos = np.arange(tq)
    with np.errstate(divide="ignore"):
        masks = np.log(np.stack([_multiplicity(pos + d * tq, pos) for d in range(nb)]))
    return pl.pallas_call(
        functools.partial(_attn_prompt_kernel, tq=tq),
        grid=(batch, nb),
        in_specs=[pl.BlockSpec((tq, D_ATT), lambda b, i: (b * nb + i, 0)),
                  pl.BlockSpec((seq, D_ATT), lambda b, i: (b, 0)),
                  pl.BlockSpec((seq, D_ATT), lambda b, i: (b, 2)),
                  pl.BlockSpec((nb, tq, tq), lambda b, i: (0, 0, 0))],
        out_specs=pl.BlockSpec((tq, D_ATT), lambda b, i: (b * nb + i, 0)),
        out_shape=jax.ShapeDtypeStruct((batch * seq, D_ATT), F32),
        compiler_params=_cparams(("parallel", "arbitrary")),
        name="attn_prompt",
    )(q, k, qkv, jnp.asarray(masks))


def _attn_sample_kernel(q_ref, kn_ref, vn_ref, kt_ref, vt_ref, mp_ref, mn_ref, hm_ref, o_ref, *, tq):
    q = q_ref[...]
    hm = hm_ref[...]
    qbd = jnp.concatenate([q] * N_HEADS, axis=0) * hm
    s_past = _dot(qbd, kt_ref[...])
    s_new = _dot_nt(qbd, kn_ref[...])
    c_past, c_new = mp_ref[...], mn_ref[...]
    sm_past = jnp.where(c_past > 0.0, s_past, NEG_BIG)
    sm_new = jnp.where(c_new > 0.0, s_new, NEG_BIG)
    m = jnp.maximum(jnp.max(sm_past, axis=1, keepdims=True), jnp.max(sm_new, axis=1, keepdims=True))
    p_past = jnp.exp(sm_past - m) * c_past
    p_new = jnp.exp(sm_new - m) * c_new
    l = jnp.sum(p_past, axis=1, keepdims=True) + jnp.sum(p_new, axis=1, keepdims=True)
    o = (_dot_nt(p_past, vt_ref[...]) + _dot(p_new, vn_ref[...])) / l * hm
    out = o[0:tq]
    for h in range(1, N_HEADS):
        out = out + o[h * tq:(h + 1) * tq]
    o_ref[...] = out


def _attn_sample(q, k, qkv, row0, kt_all, vt_all, layer, batch, tq, past):
    q_pos = past + np.arange(tq)
    rep = lambda c: np.concatenate([c] * N_HEADS, axis=0)
    mp = rep(_multiplicity(q_pos, np.arange(past)))
    mn = rep(_multiplicity(q_pos, q_pos))
    row_head = np.arange(N_HEADS * tq) // tq
    hm = (row_head[:, None] == (np.arange(D_ATT) // HEAD_DIM)[None, :]).astype(np.float32)
    rb0 = row0 // tq
    new_spec = pl.BlockSpec((tq, D_ATT), lambda b: (b, 0))
    cache_spec = pl.BlockSpec((None, None, D_ATT, past), lambda b: (layer, b, 0, 0))
    full_spec = lambda a: pl.BlockSpec(a.shape, lambda b: (0,) * a.ndim)
    mp, mn, hm = jnp.asarray(mp), jnp.asarray(mn), jnp.asarray(hm)
    return pl.pallas_call(
        functools.partial(_attn_sample_kernel, tq=tq),
        grid=(batch,),
        in_specs=[new_spec, new_spec, pl.BlockSpec((tq, D_ATT), lambda b: (rb0 + b, 2)),
                  cache_spec, cache_spec, full_spec(mp), full_spec(mn), full_spec(hm)],
        out_specs=new_spec,
        out_shape=jax.ShapeDtypeStruct((batch * tq, D_ATT), F32),
        compiler_params=_cparams(("parallel",)),
        name="attn_sample",
    )(q, k, qkv, kt_all, vt_all, mp, mn, hm)


def _rglru_kernel(x_ref, g_ref, tail0_ref, h0_ref, cw_ref, cb_ref, wa_ref, ba_ref, wx_ref, bx_ref, lam_ref,
                  y_ref, tail_ref, hl_ref, xpad, a_s, b_s, hc, *, batch, tt):
    rows = tt * batch
    pad = (CONV_W - 1) * batch
    i = pl.program_id(0)

    @pl.when(i == 0)
    def _():
        xpad[0:pad, :] = tail0_ref[...]
        hc[...] = h0_ref[...]

    @pl.when(i > 0)
    def _():
        xpad[0:pad, :] = xpad[rows:rows + pad, :]

    xpad[pad:pad + rows, :] = x_ref[...]
    xc = cb_ref[...] + cw_ref[0:1, :] * xpad[0:rows, :]
    for j in range(1, CONV_W):
        xc = xc + cw_ref[j:j + 1, :] * xpad[j * batch:j * batch + rows, :]
    r = jax.nn.sigmoid(_dot(xc, wa_ref[...]) + ba_ref[...])
    ig = jax.nn.sigmoid(_dot(xc, wx_ref[...]) + bx_ref[...])
    log_a = (-LRU_C) * r * _softplus(-lam_ref[...])
    a = jnp.exp(log_a)
    a_s[...] = a
    b_s[...] = jnp.sqrt(-jnp.tanh(log_a) * (a * a + 1.0)) * (ig * xc)

    def step(t, h):
        sl = pl.ds(pl.multiple_of(t * batch, batch), batch)
        h = a_s[sl, :] * h + b_s[sl, :]
        a_s[sl, :] = h
        return h

    h = lax.fori_loop(0, tt, step, hc[...])
    hc[...] = h
    y_ref[...] = a_s[...] * jax.nn.gelu(g_ref[...])
    tail_ref[...] = xpad[rows:rows + pad, :]
    hl_ref[...] = h


def _rglru_seq_kernel(x_ref, g_ref, tail0_ref, h0_ref, cw_ref, cb_ref, wa_ref, ba_ref, wx_ref, bx_ref, lam_ref,
                      y_ref, tail_ref, hl_ref, xprev, hc, *, tt):
    @pl.when(pl.program_id(1) == 0)
    def _():
        xprev[...] = tail0_ref[...]
        hc[...] = h0_ref[...]

    x = x_ref[...]
    ext = jnp.concatenate([xprev[...], x], axis=0)
    xc = cb_ref[...] + cw_ref[CONV_W - 1:CONV_W, :] * x
    for j in range(1, CONV_W):
        xc = xc + cw_ref[CONV_W - 1 - j:CONV_W - j, :] * pltpu.roll(ext, j, 0)[SUBLANES:]
    xprev[...] = x[tt - SUBLANES:]
    r = jax.nn.sigmoid(_dot(xc, wa_ref[...]) + ba_ref[...])
    ig = jax.nn.sigmoid(_dot(xc, wx_ref[...]) + bx_ref[...])
    log_a = (-LRU_C) * r * _softplus(-lam_ref[...])
    a = jnp.exp(log_a)
    b = jnp.sqrt(-jnp.tanh(log_a) * (a * a + 1.0)) * (ig * xc)
    row = lax.broadcasted_iota(jnp.int32, (tt, 1), 0)
    s = 1
    while s < tt:
        keep = row >= s
        b = a * jnp.where(keep, pltpu.roll(b, s, 0), 0.0) + b
        a = a * jnp.where(keep, pltpu.roll(a, s, 0), 1.0)
        s *= 2
    h = a * hc[...] + b
    hc[...] = h[tt - 1:tt]
    y_ref[...] = h * jax.nn.gelu(g_ref[...])
    tail_ref[...] = x[tt - SUBLANES:]
    hl_ref[...] = h[tt - 1:tt]


def _rglru_seq(rnn2d, conv, h0, p, batch, seq, tt=256):
    pad = SUBLANES - (CONV_W - 1)
    tail0 = jnp.pad(conv, ((0, 0), (pad, 0), (0, 0)))
    const = lambda shape: pl.BlockSpec(shape, lambda b, i: (0,) * len(shape))
    vec = const((1, D_ATT))
    per_seq = lambda rows: pl.BlockSpec((None, rows, D_ATT), lambda b, i: (b, 0, 0))
    y, tail, h_last = pl.pallas_call(
        functools.partial(_rglru_seq_kernel, tt=tt),
        grid=(batch, seq // tt),
        in_specs=[pl.BlockSpec((tt, D_ATT), lambda b, i: (i, 2 * b)),
                  pl.BlockSpec((tt, D_ATT), lambda b, i: (i, 2 * b + 1)),
                  per_seq(SUBLANES), per_seq(1), const((CONV_W, D_ATT)), vec,
                  const((D_ATT, D_ATT)), vec, const((D_ATT, D_ATT)), vec, vec],
        out_specs=[pl.BlockSpec((tt, D_ATT), lambda b, i: (i, b)), per_seq(SUBLANES), per_seq(1)],
        out_shape=[jax.ShapeDtypeStruct((seq, batch * D_ATT), F32),
                   jax.ShapeDtypeStruct((batch, SUBLANES, D_ATT), F32),
                   jax.ShapeDtypeStruct((batch, 1, D_ATT), F32)],
        scratch_shapes=[pltpu.VMEM((SUBLANES, D_ATT), F32), pltpu.VMEM((1, D_ATT), F32)],
        compiler_params=_cparams(("parallel", "arbitrary")),
        name="rglru_seq",
    )(rnn2d, rnn2d, tail0, h0.reshape(batch, 1, D_ATT), p["conv_w"], p["conv_b"].reshape(1, D_ATT),
      _block_diag(p["rg_wa"]).astype(BF16), p["rg_ba"].reshape(1, D_ATT),
      _block_diag(p["rg_wx"]).astype(BF16), p["rg_bx"].reshape(1, D_ATT), p["rg_lambda"].reshape(1, D_ATT))
    return y, tail[:, pad:, :], h_last.reshape(batch, D_ATT)


def _block_diag(w):
    nb, n, _ = w.shape
    eye = jnp.eye(nb, dtype=w.dtype)
    return (w[:, :, None, :] * eye[:, None, :, None]).reshape(nb * n, nb * n)


def _rglru(rnn_tm, tail0, h0, p, batch, seq, tt):
    rows = tt * batch
    pad = (CONV_W - 1) * batch
    row_spec = pl.BlockSpec((rows, D_ATT), lambda i: (i, 0))
    const = lambda shape: pl.BlockSpec(shape, lambda i: (0,) * len(shape))
    vec = const((1, D_ATT))
    x_tm = g_tm = rnn_tm
    return pl.pallas_call(
        functools.partial(_rglru_kernel, batch=batch, tt=tt),
        grid=(seq // tt,),
        in_specs=[row_spec, pl.BlockSpec((rows, D_ATT), lambda i: (i, 1)), const((pad, D_ATT)),
                  const((batch, D_ATT)), const((CONV_W, D_ATT)), vec,
                  const((D_ATT, D_ATT)), vec, const((D_ATT, D_ATT)), vec, vec],
        out_specs=[row_spec, const((pad, D_ATT)), const((batch, D_ATT))],
        out_shape=[jax.ShapeDtypeStruct((seq * batch, D_ATT), F32),
                   jax.ShapeDtypeStruct((pad, D_ATT), F32),
                   jax.ShapeDtypeStruct((batch, D_ATT), F32)],
        scratch_shapes=[pltpu.VMEM((pad + rows, D_ATT), F32), pltpu.VMEM((rows, D_ATT), F32),
                        pltpu.VMEM((rows, D_ATT), F32), pltpu.VMEM((batch, D_ATT), F32)],
        compiler_params=_cparams(("arbitrary",)),
        name="rglru",
    )(x_tm, g_tm, tail0, h0, p["conv_w"], p["conv_b"].reshape(1, D_ATT),
      _block_diag(p["rg_wa"]).astype(BF16), p["rg_ba"].reshape(1, D_ATT),
      _block_diag(p["rg_wx"]).astype(BF16), p["rg_bx"].reshape(1, D_ATT), p["rg_lambda"].reshape(1, D_ATT))


def _rwkv_token_math(c, prev, mu, w0, w2, a0, a2, g2, kk_w, ka_w, ones):
    cm = c + (prev - c) * mu
    r = cm[:, 0:D_ATT]
    k = cm[:, D_ATT:2 * D_ATT]
    v = cm[:, 2 * D_ATT:3 * D_ATT]
    lo = cm[:, 3 * D_ATT:3 * D_ATT + LORA_W + LORA_A]
    g_lo = cm[:, 3 * D_ATT + LORA_W + LORA_A:]
    w = -_softplus(-(w0 + _dot(jnp.tanh(lo), w2))) - 0.5
    a = jax.nn.sigmoid(a0 + _dot(lo, a2))
    g = _dot(jax.nn.sigmoid(g_lo), g2)
    kk = k * kk_w
    nrm = jnp.sqrt(_dot_split(kk * kk, ones, 2))
    return r, k * (1.0 + (a - 1.0) * ka_w), v, kk / jnp.maximum(nrm, 1e-12), a, -jnp.exp(w), g


def _rwkv_pre_kernel(c_ref, prev_ref, mu_ref, w0_ref, w2_ref, a0_ref, a2_ref, g2_ref, kk_ref, ka_ref, ones_ref,
                     *out_refs):
    vals = _rwkv_token_math(c_ref[...], prev_ref[...], mu_ref[...], w0_ref[...], w2_ref[...], a0_ref[...],
                            a2_ref[...], g2_ref[...], kk_ref[...], ka_ref[...], ones_ref[...])
    for ref, val in zip(out_refs, vals):
        ref[...] = val


def _rwkv_token_params(p):
    w2p = jnp.concatenate([p["rwkv_w2"], jnp.zeros((LORA_A, D_ATT), F32)], axis=0).astype(BF16)
    a2p = jnp.concatenate([jnp.zeros((LORA_W, D_ATT), F32), p["rwkv_a2"]], axis=0).astype(BF16)
    return (p["rwkv_mu"].reshape(1, C_COLS), p["rwkv_w0"].reshape(1, D_ATT), w2p, p["rwkv_a0"].reshape(1, D_ATT),
            a2p, p["rwkv_g2"].astype(BF16), p["rwkv_kk"].reshape(1, D_ATT), p["rwkv_ka"].reshape(1, D_ATT),
            _head_ones())


def _rwkv_pre(cc, prev, p, tm=256):
    n = cc.shape[0]
    params = _rwkv_token_params(p)
    row = pl.BlockSpec((tm, C_COLS), lambda i: (i, 0))
    const = lambda a: pl.BlockSpec(a.shape, lambda i: (0,) * a.ndim)
    out = pl.BlockSpec((tm, D_ATT), lambda i: (i, 0))
    return pl.pallas_call(
        _rwkv_pre_kernel,
        grid=(n // tm,),
        in_specs=[row, row] + [const(a) for a in params],
        out_specs=[out] * 7,
        out_shape=[jax.ShapeDtypeStruct((n, D_ATT), F32)] * 7,
        compiler_params=_cparams(("parallel",)),
        name="rwkv_pre",
    )(cc, prev, *params)


def _wkv_kernel(*refs, c, bb):
    c_refs = refs[:bb]
    (shift0_ref, s0_ref, mu_ref, w0_ref, w2_ref, a0_ref, a2_ref, g2_ref, kk_ref, ka_ref, ones_ref,
     rk_ref, lnw_ref, lnb_ref, tri_ref, sl_ref, il_ref, eye_ref, bm_ref,
     o_ref, sl_out_ref, s_scr, y_scr, last_scr) = refs[bb:]
    ci = pl.program_id(1)

    @pl.when(ci == 0)
    def _():
        s_scr[...] = s0_ref[...]
        last_scr[...] = shift0_ref[...]

    rk, lnw, lnb, tri, strict, incl, eye, bmask, ones = (
        rk_ref[...], lnw_ref[...], lnb_ref[...], tri_ref[...], sl_ref[...], il_ref[...], eye_ref[...],
        bm_ref[...], ones_ref[...])
    token_params = (mu_ref[...], w0_ref[...], w2_ref[...], a0_ref[...], a2_ref[...], g2_ref[...], kk_ref[...],
                    ka_ref[...], ones)
    first_row = lax.broadcasted_iota(jnp.int32, (bb * c, 1), 0) % c == 0
    lane = lax.broadcasted_iota(jnp.int32, (1, LANES), 1)
    first = lane < HEAD_DIM
    stack = lambda z: jnp.concatenate([jnp.where(first, z, 0.0), jnp.where(first, 0.0, z)], axis=0)
    twice = lambda z: jnp.concatenate([z, z], axis=0)
    fold = lambda z: z[:c] + z[c:]
    n_fac = int(np.log2(c))
    two_c = 2 * c

    cols = jnp.concatenate([c_refs[bi][0] for bi in range(bb)], axis=0)
    before = jnp.concatenate([jnp.broadcast_to(last_scr[bi], (c, C_COLS)) for bi in range(bb)], axis=0)
    prev = jnp.where(first_row, before, pltpu.roll(cols, 1, 0))
    for bi in range(bb):
        last_scr[bi] = cols[(bi + 1) * c - 1:(bi + 1) * c, :]
    tokens = _rwkv_token_math(cols, prev, *token_params)
    cum_all = _split_dot_left(tri, tokens[5], 3)
    seqs = []
    for bi in range(bb):
        r, k, v, kap, a, lw, g = (z[bi * c:(bi + 1) * c] for z in tokens)
        cum = cum_all[bi * c:(bi + 1) * c]
        tot = cum[c - 1:c, :]
        e_neg, e_end = jnp.exp(-cum), jnp.exp(tot - cum)
        b = kap * a
        seqs.append(dict(r=r, k=k, v=v, g=g, kap_t=kap * jnp.exp(cum - lw), r_t=r * jnp.exp(cum), k_t=k * e_neg,
                         b_t=b * e_neg, k_h=k * e_end, b_h=b * e_end, g_end=jnp.exp(tot)))
    chains = [(bi, hp, slice(hp * LANES, (hp + 1) * LANES)) for bi in range(bb) for hp in range(HEAD_PAIRS)]

    grams, states, v_stacks = [], [], []
    for bi, hp, sl in chains:
        q = seqs[bi]
        grams.append(_dot_nt(jnp.concatenate([stack(q["kap_t"][:, sl]), stack(q["r_t"][:, sl])], axis=0),
                             jnp.concatenate([twice(q["k_t"][:, sl]), twice(q["b_t"][:, sl])], axis=0)))
        states.append(s_scr[bi, hp])
        v_stacks.append(stack(q["v"][:, sl]))
    from_state = [_dot_nt(jnp.concatenate([seqs[bi]["kap_t"][:, sl], seqs[bi]["r_t"][:, sl]], axis=0), s_p)
                  for (bi, hp, sl), s_p in zip(chains, states)]
    a_kk = [gm[:two_c, :two_c] * strict for gm in grams]
    y_mat = [jnp.concatenate([gm[two_c:, :two_c] * incl, -(gm[two_c:, two_c:] * incl)], axis=1) for gm in grams]
    pw = [-(gm[:two_c, two_c:] * strict) for gm in grams]
    inv = [eye + m for m in pw]
    for _ in range(n_fac - 1):
        pw = [_dot(m, m) for m in pw]
        inv = [t + _dot(t, m) for t, m in zip(inv, pw)]
    x_s = [stack(fs[:c]) + _dot(am, vs) for fs, am, vs in zip(from_state, a_kk, v_stacks)]
    u_s = [_dot(t, x) for t, x in zip(inv, x_s)]
    y_s = [stack(fs[c:]) + _dot(ym, jnp.concatenate([vs, u], axis=0))
           for fs, ym, vs, u in zip(from_state, y_mat, v_stacks, u_s)]
    upd = [_dot(jnp.concatenate([seqs[bi]["v"][:, sl], fold(u)], axis=0).T,
                jnp.concatenate([seqs[bi]["k_h"][:, sl], -seqs[bi]["b_h"][:, sl]], axis=0))
           for (bi, hp, sl), u in zip(chains, u_s)]
    for (bi, hp, sl), s_p, d_s, y in zip(chains, states, upd, y_s):
        s_scr[bi, hp] = s_p * seqs[bi]["g_end"][:, sl] + bmask * d_s
        y_scr[bi * c:(bi + 1) * c, sl] = fold(y)

    r, k, v, g = tokens[0], tokens[1], tokens[2], tokens[6]
    inv_n = 1.0 / HEAD_DIM
    y = y_scr[...]
    mean = _dot_split(y, ones, 2) * inv_n
    d = y - mean
    var = _dot_split(d * d, ones, 2) * inv_n
    yn = d * lax.rsqrt(var + GN_EPS) * lnw + lnb
    out = (yn + _dot_split(r * k * rk, ones, 2) * v) * g
    for bi in range(bb):
        o_ref[bi, 0] = out[bi * c:(bi + 1) * c]

    @pl.when(ci == pl.num_programs(1) - 1)
    def _():
        sl_out_ref[...] = s_scr[...]


def _wkv(cc, shift0, s0_pairs, p, batch, seq, c=64, bb=4):
    assert 2 * c == LANES
    nc = seq // c
    cc3 = cc.reshape(batch * nc, c, C_COLS)
    token_params = _rwkv_token_params(p)
    idx = np.arange(2 * c)
    same = (idx[:, None] // c) == (idx[None, :] // c)
    strict = (same & (idx[None, :] % c < idx[:, None] % c)).astype(np.float32)
    incl = (same & (idx[None, :] % c <= idx[:, None] % c)).astype(np.float32)
    eye = np.eye(2 * c, dtype=np.float32)
    half = np.arange(LANES) // HEAD_DIM
    bmask = (half[:, None] == half[None, :]).astype(np.float32)
    rows = np.arange(bb * c)
    tri = ((rows[None, :] <= rows[:, None]) & (rows[None, :] // c == rows[:, None] // c)).astype(np.float32)

    def chunk(j):
        return pl.BlockSpec((1, c, C_COLS), lambda g, i: ((g * bb + j) * nc + i, 0, 0))

    const = lambda a: pl.BlockSpec(a.shape, lambda g, i: (0,) * a.ndim)
    state = pl.BlockSpec((bb, HEAD_PAIRS, LANES, LANES), lambda g, i: (g, 0, 0, 0))
    consts = token_params + (p["rwkv_rk"].reshape(1, D_ATT), p["rwkv_lnw"].reshape(1, D_ATT),
                             p["rwkv_lnb"].reshape(1, D_ATT), jnp.asarray(tri, dtype=BF16), jnp.asarray(strict),
                             jnp.asarray(incl), jnp.asarray(eye), jnp.asarray(bmask))
    return pl.pallas_call(
        functools.partial(_wkv_kernel, c=c, bb=bb),
        grid=(batch // bb, nc),
        in_specs=[chunk(j) for j in range(bb)]
        + [pl.BlockSpec((bb, 1, C_COLS), lambda g, i: (g, 0, 0)), state] + [const(a) for a in consts],
        out_specs=[pl.BlockSpec((bb, 1, c, D_ATT), lambda g, i: (g, i, 0, 0)), state],
        out_shape=[jax.ShapeDtypeStruct((batch, nc, c, D_ATT), F32),
                   jax.ShapeDtypeStruct((batch, HEAD_PAIRS, LANES, LANES), F32)],
        scratch_shapes=[pltpu.VMEM((bb, HEAD_PAIRS, LANES, LANES), F32), pltpu.VMEM((bb * c, D_ATT), F32),
                        pltpu.VMEM((bb, 1, C_COLS), F32)],
        compiler_params=_cparams(("parallel", "arbitrary")),
        name="wkv",
    )(*([cc3] * bb), shift0.reshape(batch, 1, C_COLS), s0_pairs, *consts)


def _wkv_lane_kernel(r_ref, k_ref, v_ref, kap_ref, a_ref, lw_ref, g_ref, s0_ref, rk_ref, lnw_ref, lnb_ref,
                     o_ref, s_ref, *, steps):
    s_ref[...] = s0_ref[...]
    inv_n = 1.0 / HEAD_DIM
    for t in range(steps):
        kap = kap_ref[t]
        w = jnp.exp(lw_ref[t])
        bt = kap * a_ref[t]
        kt = k_ref[t]
        rt = r_ref[t]

        def per_v(vi, _, t=t, kap=kap, w=w, bt=bt, kt=kt, rt=rt):
            s = s_ref[vi]
            sa = -jnp.sum(s * kap, axis=0, keepdims=True)
            vv = v_ref[t, pl.ds(vi, 1), :]
            s = s * w + sa * bt + vv * kt
            s_ref[vi] = s
            o_ref[t, pl.ds(vi, 1), :] = jnp.sum(s * rt, axis=0, keepdims=True)
            return 0

        lax.fori_loop(0, HEAD_DIM, per_v, 0, unroll=4)
        y = o_ref[t]
        mean = jnp.sum(y, axis=0, keepdims=True) * inv_n
        d = y - mean
        var = jnp.sum(d * d, axis=0, keepdims=True) * inv_n
        yn = d * lax.rsqrt(var + GN_EPS) * lnw_ref[...] + lnb_ref[...]
        bonus = jnp.sum(rt * kt * rk_ref[...], axis=0, keepdims=True) * v_ref[t]
        o_ref[t] = (yn + bonus) * g_ref[t]


def _wkv_lane(parts_t, s0_t, p, batch, steps):
    tok = pl.BlockSpec((steps, HEAD_DIM, batch), lambda h: (0, h, 0))
    st = pl.BlockSpec((HEAD_DIM, HEAD_DIM, batch), lambda h: (h, 0, 0))
    col = pl.BlockSpec((HEAD_DIM, batch), lambda h: (h, 0))
    bcast = lambda z: jnp.broadcast_to(z.reshape(D_ATT, 1), (D_ATT, batch))
    return pl.pallas_call(
        functools.partial(_wkv_lane_kernel, steps=steps),
        grid=(N_HEADS,),
        in_specs=[tok] * 7 + [st, col, col, col],
        out_specs=[tok, st],
        out_shape=[jax.ShapeDtypeStruct((steps, D_ATT, batch), F32),
                   jax.ShapeDtypeStruct((D_ATT, HEAD_DIM, batch), F32)],
        compiler_params=_cparams(("parallel",)),
        name="wkv_lane",
    )(*parts_t, s0_t, bcast(p["rwkv_rk"]), bcast(p["rwkv_lnw"]), bcast(p["rwkv_lnb"]))


def _pair_states(s):
    b = s.shape[0]
    s = s.reshape(b, HEAD_PAIRS, 2, HEAD_DIM, HEAD_DIM)
    z = jnp.zeros((b, HEAD_PAIRS, HEAD_DIM, HEAD_DIM), s.dtype)
    top = jnp.concatenate([s[:, :, 0], z], axis=-1)
    bot = jnp.concatenate([z, s[:, :, 1]], axis=-1)
    return jnp.concatenate([top, bot], axis=-2)


def _unpair_states(sp):
    b = sp.shape[0]
    first = sp[:, :, :HEAD_DIM, :HEAD_DIM]
    second = sp[:, :, HEAD_DIM:, HEAD_DIM:]
    return jnp.stack([first, second], axis=2).reshape(b, N_HEADS, HEAD_DIM, HEAD_DIM)


def _merge_kernel(oa_ref, ob_ref, oc_ref, gl_ref, x_ref, bm_ref, wa_ref, wb_ref, wc_ref, wo_ref, nw_ref,
                  wgh_ref, wgl_ref, bg_ref, x1_ref, xn_ref, comb_ref):
    merged = None
    for j, (o_ref, w_ref) in enumerate(((oa_ref, wa_ref), (ob_ref, wb_ref), (oc_ref, wc_ref))):
        cols = slice(j * D_MODEL, (j + 1) * D_MODEL)
        gate = jax.nn.sigmoid(gl_ref[:, cols] + bm_ref[:, cols])
        t = gate * _dot(o_ref[...], w_ref[...])
        merged = t if merged is None else merged + t
    x1 = x_ref[...] + _dot(merged, wo_ref[...])
    x1_ref[...] = x1
    xn = x1 * lax.rsqrt(jnp.mean(x1 * x1, axis=-1, keepdims=True) + EPS) * nw_ref[...]
    xn_ref[...] = xn.astype(BF16)
    hi = xn.astype(BF16)
    lo = (xn - hi.astype(F32)).astype(BF16)
    wgh, wgl = wgh_ref[...], wgl_ref[...]
    lg = (jnp.dot(hi, wgh, preferred_element_type=F32) + jnp.dot(lo, wgh, preferred_element_type=F32)
          + jnp.dot(hi, wgl, preferred_element_type=F32)) + bg_ref[...]
    lane = lax.broadcasted_iota(jnp.int32, lg.shape, 1).astype(F32)
    none = float(LANES)
    row_max = lambda z: jnp.max(z, axis=1, keepdims=True)
    first_lane = lambda hit: jnp.min(jnp.where(hit, lane, none), axis=1, keepdims=True)
    g_logit = jnp.where(lane < N_GROUPS, lg, -jnp.inf)
    best = row_max(g_logit)
    gi = first_lane(g_logit == best)
    p_group = 1.0 / jnp.sum(jnp.exp(g_logit - best), axis=1, keepdims=True)
    base = N_GROUPS + gi * EXP_PER_GROUP
    e_logit = jnp.where(lane >= base, jnp.where(lane < base + EXP_PER_GROUP, lg, -jnp.inf), -jnp.inf)
    v1 = row_max(e_logit)
    l1 = first_lane(e_logit == v1)
    rest = jnp.where(lane == l1, -jnp.inf, e_logit)
    v2 = row_max(rest)
    l2 = first_lane(rest == v2)
    e2 = jnp.exp(v2 - v1)
    w1 = p_group / (1.0 + e2)
    w2 = p_group * e2 / (1.0 + e2)
    comb_ref[...] = jnp.where(lane == l1 - N_GROUPS, w1, 0.0) + jnp.where(lane == l2 - N_GROUPS, w2, 0.0)


def _merge(oa, ob, oc, gl, x, p, wb, tm=512, ob_time_major=None):
    n = x.shape[0]
    ob_spec = pl.BlockSpec((tm, D_ATT), lambda i: (i, 0))
    if ob_time_major is not None:
        batch, seq = ob_time_major
        nt = seq // tm
        ob = ob.reshape(seq, batch * D_ATT)
        ob_spec = pl.BlockSpec((tm, D_ATT), lambda i: (i % nt, i // nt))
    wg = jnp.concatenate([p["moe_wg_group"], p["moe_wg_exp"]], axis=1)
    wg = jnp.pad(wg, ((0, 0), (0, LANES - wg.shape[1])))
    wgh = wg.astype(BF16)
    wgl = (wg - wgh.astype(F32)).astype(BF16)
    bg = jnp.pad(jnp.concatenate([p["moe_bg_group"], p["moe_bg_exp"]]), (0, LANES - N_GROUPS - N_EXPERTS))
    row = lambda w: pl.BlockSpec((tm, w), lambda i: (i, 0))
    const = lambda shape: pl.BlockSpec(shape, lambda i: (0,) * len(shape))
    return pl.pallas_call(
        _merge_kernel,
        grid=(n // tm,),
        in_specs=[row(D_ATT), ob_spec, row(D_ATT), row(GL_COLS), row(D_MODEL), const((1, GL_COLS)),
                  const((D_ATT, D_MODEL)), const((D_ATT, D_MODEL)), const((D_ATT, D_MODEL)),
                  const((D_MODEL, D_MODEL)), const((1, D_MODEL)), const((D_MODEL, LANES)),
                  const((D_MODEL, LANES)), const((1, LANES))],
        out_specs=[row(D_MODEL), row(D_MODEL), row(LANES)],
        out_shape=[jax.ShapeDtypeStruct((n, D_MODEL), F32), jax.ShapeDtypeStruct((n, D_MODEL), BF16),
                   jax.ShapeDtypeStruct((n, LANES), F32)],
        compiler_params=_cparams(("parallel",)),
        name="merge",
    )(oa, ob, oc, gl, x, p["b_merge"].reshape(1, GL_COLS), wb["w_branch_a"], wb["w_branch_b"], wb["w_branch_c"],
      wb["w_out"], p["norm2_w"].reshape(1, D_MODEL), wgh, wgl, bg.reshape(1, LANES))


MOE_PAIR = 2


def _moe_kernel(xn_ref, comb_ref, x1_ref, w1_ref, w3_ref, w2_ref, o_ref, acc):
    e = pl.program_id(1)

    @pl.when(e == 0)
    def _():
        acc[...] = x1_ref[...]

    xb = xn_ref[...]
    lane = lax.broadcasted_iota(jnp.int32, comb_ref.shape, 1)
    comb = comb_ref[...]
    pick = lambda j: jnp.sum(jnp.where(lane == MOE_PAIR * e + j, comb, 0.0), axis=1, keepdims=True)
    hidden = []
    for j in range(MOE_PAIR):
        h = jax.nn.silu(jnp.dot(xb, w1_ref[j], preferred_element_type=F32)) * jnp.dot(xb, w3_ref[j],
                                                                                      preferred_element_type=F32)
        hidden.append((h * pick(j)).astype(BF16))
    acc[...] += jnp.dot(jnp.concatenate(hidden, axis=1), w2_ref[0], preferred_element_type=F32)

    @pl.when(e == pl.num_programs(1) - 1)
    def _():
        o_ref[...] = acc[...]


def _moe_weights(w1, w3, w2):
    depth = w2.shape[0]
    return (w1.astype(BF16), w3.astype(BF16),
            w2.astype(BF16).reshape(depth, N_EXPERTS // MOE_PAIR, MOE_PAIR * D_EXPERT, D_MODEL))


def _moe(xn, comb, x1, w1, w3, w2, layer, tm=1024):
    n = xn.shape[0]
    hid = MOE_PAIR * D_EXPERT
    return pl.pallas_call(
        _moe_kernel,
        grid=(n // tm, N_EXPERTS // MOE_PAIR),
        in_specs=[pl.BlockSpec((tm, D_MODEL), lambda i, e: (i, 0)),
                  pl.BlockSpec((tm, LANES), lambda i, e: (i, 0)),
                  pl.BlockSpec((tm, D_MODEL), lambda i, e: (i, 0)),
                  pl.BlockSpec((None, MOE_PAIR, D_MODEL, D_EXPERT), lambda i, e: (layer, e, 0, 0)),
                  pl.BlockSpec((None, MOE_PAIR, D_MODEL, D_EXPERT), lambda i, e: (layer, e, 0, 0)),
                  pl.BlockSpec((None, 1, hid, D_MODEL), lambda i, e: (layer, e, 0, 0))],
        out_specs=pl.BlockSpec((tm, D_MODEL), lambda i, e: (i, 0)),
        out_shape=jax.ShapeDtypeStruct((n, D_MODEL), F32),
        scratch_shapes=[pltpu.VMEM((tm, D_MODEL), F32)],
        compiler_params=_cparams(("parallel", "arbitrary")),
        name="moe",
    )(xn, comb, x1, w1, w3, w2)


def _time_major(z, batch, seq):
    return z.reshape(batch, seq, z.shape[-1]).transpose(1, 0, 2).reshape(seq * batch, z.shape[-1])


def _batch_major(z, batch, seq):
    return z.reshape(seq, batch, z.shape[-1]).transpose(1, 0, 2).reshape(batch * seq, z.shape[-1])


def _group_layer(x, g, p, wb, caches, layer, tabs):
    batch, seq = g["batch"], g["seq"]
    rows = batch * seq
    prompt = g["past"] == 0
    tm = 256
    qkv, rnn, cc, gl = _inproj(x, p["norm1_w"], wb["w_in"], layer,
                               time_major_rnn=(batch, seq) if prompt else None)
    heads = lambda z: z.reshape(batch, seq, N_HEADS, HEAD_DIM)
    if prompt:
        q, k, kt, vt = _qkprep(qkv, 0, rows, tabs, g["table_blocks"], p["q_norm_w"], p["k_norm_w"], tm,
                               cache_batch=batch)
        oa = _attn_prompt(q, k, qkv, batch, seq)
        from_t = lambda z: z.reshape(batch, N_HEADS, HEAD_DIM, seq).transpose(0, 3, 1, 2)
        k_new, v_new = from_t(kt), from_t(vt)
        ob, conv_tail, h_last = _rglru_seq(rnn, g["conv"], g["h"], p, batch, seq)
        o_wkv, s_pairs = _wkv(cc, g["shift"], _pair_states(g["wkv"]), p, batch, seq)
        oc = o_wkv.reshape(rows, D_ATT)
        s_last = _unpair_states(s_pairs)
    else:
        q, k = _qkprep(qkv, 0, rows, tabs, g["table_blocks"], p["q_norm_w"], p["k_norm_w"], tm)
        oa = _attn_sample(q, k, qkv, 0, caches[0], caches[1], layer, batch, seq, g["past"])
        k_new, v_new = heads(k), heads(qkv[:, 2 * D_ATT:])
        tail0 = g["conv"].transpose(1, 0, 2).reshape((CONV_W - 1) * batch, D_ATT)
        y_tm, tail, h_last = _rglru(_time_major(rnn, batch, seq), tail0, g["h"], p, batch, seq, g["tt"])
        ob = _batch_major(y_tm, batch, seq)
        conv_tail = tail.reshape(CONV_W - 1, batch, D_ATT).transpose(1, 0, 2)
        c3 = cc.reshape(batch, seq, C_COLS)
        prev = jnp.concatenate([g["shift"][:, None, :], c3[:, :-1]], axis=1).reshape(rows, C_COLS)
        to_lanes = lambda z: z.reshape(batch, seq, D_ATT).transpose(1, 2, 0)
        o_t, s_t = _wkv_lane([to_lanes(z) for z in _rwkv_pre(cc, prev, p)], g["wkv"], p, batch, seq)
        oc = o_t.transpose(2, 0, 1).reshape(rows, D_ATT)
        s_last = s_t.reshape(N_HEADS, HEAD_DIM, HEAD_DIM, batch).transpose(3, 0, 1, 2)
    shift_last = cc.reshape(batch, seq, C_COLS)[:, -1]
    x1, xn2, comb = _merge(oa, ob, oc, gl, x, p, wb, ob_time_major=(batch, seq) if prompt else None)
    x2 = _moe(xn2, comb, x1, wb["moe_w1"], wb["moe_w3"], wb["moe_w2"], layer, tm=min(1024, rows))
    return x2, (k_new, v_new, conv_tail, h_last, shift_last, s_last)


def kernel(x_prompt, x_sample, cache_k_win, cache_v_win, state_conv, state_h, state_shift, state_wkv, norm1_w, w_in, q_norm_w, k_norm_w, conv_w, conv_b, rg_wa, rg_ba, rg_wx, rg_bx, rg_lambda, rwkv_mu, rwkv_w0, rwkv_w2, rwkv_a0, rwkv_a2, rwkv_g2, rwkv_kk, rwkv_ka, rwkv_rk, rwkv_lnw, rwkv_lnb, b_merge, w_branch_a, w_branch_b, w_branch_c, w_out, norm2_w, moe_wg_group, moe_bg_group, moe_wg_exp, moe_bg_exp, moe_w1, moe_w3, moe_w2):
    weights = dict(norm1_w=norm1_w, w_in=w_in, q_norm_w=q_norm_w, k_norm_w=k_norm_w, conv_w=conv_w, conv_b=conv_b,
                   rg_wa=rg_wa, rg_ba=rg_ba, rg_wx=rg_wx, rg_bx=rg_bx, rg_lambda=rg_lambda, rwkv_mu=rwkv_mu,
                   rwkv_w0=rwkv_w0, rwkv_w2=rwkv_w2, rwkv_a0=rwkv_a0, rwkv_a2=rwkv_a2, rwkv_g2=rwkv_g2,
                   rwkv_kk=rwkv_kk, rwkv_ka=rwkv_ka, rwkv_rk=rwkv_rk, rwkv_lnw=rwkv_lnw, rwkv_lnb=rwkv_lnb,
                   b_merge=b_merge, w_branch_a=w_branch_a, w_branch_b=w_branch_b, w_branch_c=w_branch_c,
                   w_out=w_out, norm2_w=norm2_w, moe_wg_group=moe_wg_group, moe_bg_group=moe_bg_group,
                   moe_wg_exp=moe_wg_exp, moe_bg_exp=moe_bg_exp, moe_w1=moe_w1, moe_w3=moe_w3, moe_w2=moe_w2)
    depth = w_in.shape[0]
    bp, tp = x_prompt.shape[0], x_prompt.shape[1]
    bs, ts = x_sample.shape[0], x_sample.shape[1]
    past = cache_k_win.shape[2]
    x_p = x_prompt.reshape(bp * tp, D_MODEL)
    x_s = x_sample.reshape(bs * ts, D_MODEL)
    tm = 256
    tab_p = _rope_tables(jnp.arange(tp, dtype=jnp.int32))
    tab_s = tuple(jnp.tile(t, (tm // ts, 1)) for t in _rope_tables(past + jnp.arange(ts, dtype=jnp.int32)))
    kt_all = cache_k_win.transpose(0, 1, 3, 4, 2).reshape(depth, bs, D_ATT, past)
    vt_all = cache_v_win.transpose(0, 1, 3, 4, 2).reshape(depth, bs, D_ATT, past)
    wkv_t = state_wkv.transpose(0, 2, 3, 4, 1).reshape(depth, D_ATT, HEAD_DIM, bs)
    outs_p = [[] for _ in range(6)]
    outs_s = [[] for _ in range(6)]
    w_in_b = w_in.astype(BF16)
    moe_b = _moe_weights(moe_w1, moe_w3, moe_w2)
    for l in range(depth):
        p = {k_: v_[l] for k_, v_ in weights.items()}
        wb = {k_: p[k_].astype(BF16) for k_ in ("w_branch_a", "w_branch_b", "w_branch_c", "w_out")}
        wb["w_in"] = w_in_b
        wb["moe_w1"], wb["moe_w3"], wb["moe_w2"] = moe_b
        g_p = dict(batch=bp, seq=tp, past=0, table_blocks=tp // tm, tt=64,
                   conv=jnp.zeros((bp, CONV_W - 1, D_ATT), F32), h=jnp.zeros((bp, D_ATT), F32),
                   shift=jnp.zeros((bp, C_COLS), F32), wkv=jnp.zeros((bp, N_HEADS, HEAD_DIM, HEAD_DIM), F32))
        g_s = dict(batch=bs, seq=ts, past=past, table_blocks=1, tt=ts,
                   conv=state_conv[l], h=state_h[l], shift=state_shift[l], wkv=wkv_t[l])
        x_p, st_p = _group_layer(x_p, g_p, p, wb, None, l, tab_p)
        x_s, st_s = _group_layer(x_s, g_s, p, wb, (kt_all, vt_all), l, tab_s)
        for j in range(6):
            outs_p[j].append(st_p[j])
            outs_s[j].append(st_s[j])
    y_prompt = x_p.reshape(bp, tp, D_MODEL)
    y_sample = x_s.reshape(bs, ts, D_MODEL)
    stack = lambda zs: jnp.stack(zs)
    return (y_prompt, y_sample, *[stack(z) for z in outs_p], *[stack(z) for z in outs_s])
```
